```python
import jax, jax.numpy as jnp
from jax import lax
import numpy as np

D_MODEL = 4096
BATCH = 8
SEQ = 2048
DEPTH = 2
DEC_BATCH = 8
DEC_SEQ = 16
PAST_LEN = 1024

CHUNK = 64
HEAD_DIM = 128
N_HEADS_TOTAL = D_MODEL // HEAD_DIM
N_HEADS_A = N_HEADS_TOTAL // 2
N_HEADS_B = N_HEADS_TOTAL - N_HEADS_A
N_HEADS_C = N_HEADS_TOTAL
WIDTH_A = N_HEADS_A * HEAD_DIM
WIDTH_B = N_HEADS_B * HEAD_DIM
WIDTH_C = N_HEADS_C * HEAD_DIM
N_IDX_HEADS = 16
IDX_DIM = 64
TOPK_MAX = 256
A_QBLOCK = 16
B_QBLOCK = 128
C_PAST_CHUNKS = 8
C_BAND = (C_PAST_CHUNKS + 1) * CHUNK
REL_CLIP = 64
N_EVEN = (DEPTH + 1) // 2
N_ODD = DEPTH // 2
EVEN_SPLITS = (WIDTH_A, WIDTH_A, WIDTH_A, WIDTH_A, N_IDX_HEADS * IDX_DIM, IDX_DIM, N_IDX_HEADS, WIDTH_B, WIDTH_B, WIDTH_B, WIDTH_B)
EVEN_IN = sum(EVEN_SPLITS)
ODD_IN = 4 * WIDTH_C
EPS = 1e-6

kernel_name = 'hybrid_stream_dsa_stickbreak_chunkband'


def rmsnorm(x, g):
    xf = x.astype(jnp.float32)
    y = xf * lax.rsqrt(jnp.mean(xf * xf, axis=-1, keepdims=True) + EPS)
    return (y * g.astype(jnp.float32)).astype(x.dtype)


def to_blocks(a, blk):
    b, t = a.shape[:2]
    return jnp.moveaxis(a.reshape(b, t // blk, blk, *a.shape[2:]), 1, 0)


def from_blocks(a):
    a = jnp.moveaxis(a, 0, 1)
    return a.reshape(a.shape[0], a.shape[1] * a.shape[2], *a.shape[3:])


def alibi_slopes(n):
    return 2.0 ** (-8.0 * jnp.arange(1, n + 1, dtype=jnp.float32) / n)


def even_proj(h, w_in, gk):
    b, t = h.shape[:2]
    points, acc = [], 0
    for size in EVEN_SPLITS[:-1]:
        acc += size
        points.append(acc)
    qa, ka, va, ga, iq, ik, iw, qb, kb, vb, gb = jnp.split(h @ w_in, points, axis=-1)
    hd = lambda a, n: a.reshape(b, t, n, HEAD_DIM)
    return (hd(qa, N_HEADS_A), hd(ka, N_HEADS_A), hd(va, N_HEADS_A), ga,
            iq.reshape(b, t, N_IDX_HEADS, IDX_DIM), rmsnorm(ik, gk), iw,
            hd(qb, N_HEADS_B), hd(kb, N_HEADS_B), hd(vb, N_HEADS_B), gb)


def even_out(oa, ga, ob, gb, w_out):
    b, t = oa.shape[:2]
    mixed = jnp.concatenate([oa.reshape(b, t, WIDTH_A) * jax.nn.silu(ga),
                             ob.reshape(b, t, WIDTH_B) * jax.nn.silu(gb)], axis=-1)
    return mixed @ w_out


def dsa_block(q, iq, iw, qpos, k, v, kidx, kpos, top_k):
    f32 = jnp.float32
    qk_idx = jnp.einsum('bqhd,bsd->bqhs', iq, kidx, preferred_element_type=f32) * (IDX_DIM ** -0.5)
    score = jnp.einsum('bqhs,bqh->bqs', jax.nn.relu(qk_idx), iw.astype(f32)) * (N_IDX_HEADS ** -0.5)
    adm = (kpos[None, :] // CHUNK) <= (qpos[:, None] // CHUNK)
    adm = jnp.broadcast_to(adm[None], score.shape)
    _, sel = lax.top_k(jnp.where(adm, score, -jnp.inf), top_k)
    valid = jnp.take_along_axis(adm, sel, axis=-1)
    gather = jax.vmap(lambda a, i: a[i])
    kg = gather(k, sel)
    vg = gather(v, sel)
    logits = jnp.einsum('bqhd,bqkhd->bhqk', q, kg, preferred_element_type=f32) * (HEAD_DIM ** -0.5)
    dist = jnp.abs(qpos[None, :, None] - kpos[sel]).astype(f32)
    logits = logits - alibi_slopes(N_HEADS_A)[None, :, None, None] * dist[:, None]
    logits = jnp.where(valid[:, None], logits, -jnp.inf)
    p = jax.nn.softmax(logits, axis=-1)
    return jnp.einsum('bhqk,bqkhd->bqhd', p.astype(vg.dtype), vg)


def stick_block(q, qpos, k, v, kpos):
    z = jnp.einsum('bqhd,bshd->bhqs', q, k, preferred_element_type=jnp.float32) * (HEAD_DIM ** -0.5)
    causal = (kpos[None, :] < qpos[:, None])[None, None]
    log_beta = jax.nn.log_sigmoid(z)
    log_keep = jnp.where(causal, jax.nn.log_sigmoid(-z), 0.0)
    suffix = lax.cumsum(log_keep, axis=3, reverse=True) - log_keep
    w = jnp.where(causal, jnp.exp(log_beta + suffix), 0.0)
    return jnp.einsum('bhqs,bshd->bqhd', w.astype(v.dtype), v)


def band_block(q, qpos, k, v, kpos, rel_bias):
    logits = jnp.einsum('bqhd,bkhd->bhqk', q, k, preferred_element_type=jnp.float32) * (HEAD_DIM ** -0.5)
    qc = qpos[:, None] // CHUNK
    kc = kpos[None, :] // CHUNK
    allowed = (kc <= qc) & (kc >= qc - C_PAST_CHUNKS) & (kpos[None, :] >= 0)
    rel = jnp.clip(qpos[:, None] - kpos[None, :], -REL_CLIP, REL_CLIP) + REL_CLIP
    logits = logits + rel_bias[:, rel].astype(jnp.float32)[None]
    logits = jnp.where(allowed[None, None], logits, -jnp.inf)
    p = jax.nn.softmax(logits, axis=-1)
    return jnp.einsum('bhqk,bkhd->bqhd', p.astype(v.dtype), v)


def even_prompt(h, w_in, gk, w_out):
    s = h.shape[1]
    qa, ka, va, ga, iq, ik, iw, qb, kb, vb, gb = even_proj(h, w_in, gk)
    pos = jnp.arange(s)
    top_k = min(TOPK_MAX, s // 4)
    oa = from_blocks(lax.map(
        lambda a: dsa_block(a[0], a[1], a[2], a[3], ka, va, ik, pos, top_k),
        (to_blocks(qa, A_QBLOCK), to_blocks(iq, A_QBLOCK), to_blocks(iw, A_QBLOCK), pos.reshape(-1, A_QBLOCK))))
    ob = from_blocks(lax.map(
        lambda a: stick_block(a[0], a[1], kb, vb, pos),
        (to_blocks(qb, B_QBLOCK), pos.reshape(-1, B_QBLOCK))))
    y = even_out(oa, ga, ob, gb, w_out)
    return y, jnp.stack([ka, va], axis=2), ik, jnp.stack([kb, vb], axis=2)


def even_sample(h, a_kv, a_kidx, b_kv, w_in, gk, w_out):
    p_len, t = a_kv.shape[1], h.shape[1]
    qa, ka, va, ga, iq, ik, iw, qb, kb, vb, gb = even_proj(h, w_in, gk)
    qpos = p_len + jnp.arange(t)
    kpos = jnp.arange(p_len + t)
    ka_all = jnp.concatenate([a_kv[:, :, 0], ka], axis=1)
    va_all = jnp.concatenate([a_kv[:, :, 1], va], axis=1)
    ik_all = jnp.concatenate([a_kidx, ik], axis=1)
    kb_all = jnp.concatenate([b_kv[:, :, 0], kb], axis=1)
    vb_all = jnp.concatenate([b_kv[:, :, 1], vb], axis=1)
    top_k = min(TOPK_MAX, (p_len + t) // 4)
    oa = dsa_block(qa, iq, iw, qpos, ka_all, va_all, ik_all, kpos, top_k)
    ob = stick_block(qb, qpos, kb_all, vb_all, kpos)
    y = even_out(oa, ga, ob, gb, w_out)
    return y, jnp.stack([ka, va], axis=2), ik, jnp.stack([kb, vb], axis=2)


def odd_proj(h, w_in):
    b, t = h.shape[:2]
    q, k, v, g = jnp.split(h @ w_in, 4, axis=-1)
    hd = lambda a: a.reshape(b, t, N_HEADS_C, HEAD_DIM)
    return hd(q), hd(k), hd(v), g


def odd_prompt(h, w_in, rel_bias, w_out):
    b, s = h.shape[:2]
    q, k, v, g = odd_proj(h, w_in)
    pad = C_PAST_CHUNKS * CHUNK
    kp = jnp.pad(k, ((0, 0), (pad, 0), (0, 0), (0, 0)))
    vp = jnp.pad(v, ((0, 0), (pad, 0), (0, 0), (0, 0)))

    def chunk_fn(c):
        start = c * CHUNK
        qc = lax.dynamic_slice_in_dim(q, start, CHUNK, axis=1)
        kc = lax.dynamic_slice_in_dim(kp, start, C_BAND, axis=1)
        vc = lax.dynamic_slice_in_dim(vp, start, C_BAND, axis=1)
        qpos = start + jnp.arange(CHUNK)
        kpos = start - pad + jnp.arange(C_BAND)
        return band_block(qc, qpos, kc, vc, kpos, rel_bias)

    o = from_blocks(lax.map(chunk_fn, jnp.arange(s // CHUNK)))
    y = (o.reshape(b, s, WIDTH_C) * jax.nn.silu(g)) @ w_out
    wc = min(pad, s)
    return y, jnp.stack([k[:, s - wc:], v[:, s - wc:]], axis=2)


def odd_sample(h, c_kv, p_len, w_in, rel_bias, w_out):
    b, t = h.shape[:2]
    q, k, v, g = odd_proj(h, w_in)
    wc = c_kv.shape[1]
    k_all = jnp.concatenate([c_kv[:, :, 0], k], axis=1)
    v_all = jnp.concatenate([c_kv[:, :, 1], v], axis=1)
    kpos = p_len - wc + jnp.arange(wc + t)
    qpos = p_len + jnp.arange(t)
    o = band_block(q, qpos, k_all, v_all, kpos, rel_bias)
    y = (o.reshape(b, t, WIDTH_C) * jax.nn.silu(g)) @ w_out
    return y, jnp.stack([k, v], axis=2)


def setup_inputs(seed: int = 0) -> dict:
    key = jax.random.key(seed)
    ks = jax.random.split(key, 16)
    nrm = jax.random.normal
    wc = min(C_PAST_CHUNKS * CHUNK, PAST_LEN)
    return {
        'x_prompt': nrm(ks[0], (BATCH, SEQ, D_MODEL), jnp.float32),
        'x_sample': nrm(ks[1], (DEC_BATCH, DEC_SEQ, D_MODEL), jnp.float32),
        'cache_a_kv': nrm(ks[2], (N_EVEN, DEC_BATCH, PAST_LEN, 2, N_HEADS_A, HEAD_DIM), jnp.float32),
        'cache_a_kidx': nrm(ks[3], (N_EVEN, DEC_BATCH, PAST_LEN, IDX_DIM), jnp.float32),
        'cache_b_kv': nrm(ks[4], (N_EVEN, DEC_BATCH, PAST_LEN, 2, N_HEADS_B, HEAD_DIM), jnp.float32),
        'cache_c_kv': nrm(ks[5], (N_ODD, DEC_BATCH, wc, 2, N_HEADS_C, HEAD_DIM), jnp.float32),
        'norm_e': 1.0 + 0.02 * nrm(ks[6], (N_EVEN, D_MODEL), jnp.float32),
        'w_in_e': nrm(ks[7], (N_EVEN, D_MODEL, EVEN_IN), jnp.float32) * D_MODEL ** -0.5,
        'idx_k_gain': 1.0 + 0.02 * nrm(ks[8], (N_EVEN, IDX_DIM), jnp.float32),
        'w_out_e': nrm(ks[9], (N_EVEN, WIDTH_A + WIDTH_B, D_MODEL), jnp.float32) * (WIDTH_A + WIDTH_B) ** -0.5,
        'norm_o': 1.0 + 0.02 * nrm(ks[10], (N_ODD, D_MODEL), jnp.float32),
        'w_in_o': nrm(ks[11], (N_ODD, D_MODEL, ODD_IN), jnp.float32) * D_MODEL ** -0.5,
        'rel_bias_o': 0.5 * nrm(ks[12], (N_ODD, N_HEADS_C, 2 * REL_CLIP + 1), jnp.float32),
        'w_out_o': nrm(ks[13], (N_ODD, WIDTH_C, D_MODEL), jnp.float32) * WIDTH_C ** -0.5,
        'norm_f': 1.0 + 0.02 * nrm(ks[14], (D_MODEL,), jnp.float32),
    }


def reference(x_prompt, x_sample, cache_a_kv, cache_a_kidx, cache_b_kv, cache_c_kv,
              norm_e, w_in_e, idx_k_gain, w_out_e, norm_o, w_in_o, rel_bias_o, w_out_o, norm_f):
    p_len = cache_a_kv.shape[2]
    xp, xs = x_prompt, x_sample
    a_kv_p, a_ix_p, b_kv_p, c_kv_p = [], [], [], []
    a_kv_s, a_ix_s, b_kv_s, c_kv_s = [], [], [], []
    for layer in range(DEPTH):
        j = layer // 2
        if layer % 2 == 0:
            yp, akv, aix, bkv = even_prompt(rmsnorm(xp, norm_e[j]), w_in_e[j], idx_k_gain[j], w_out_e[j])
            ys, akv2, aix2, bkv2 = even_sample(rmsnorm(xs, norm_e[j]), cache_a_kv[j], cache_a_kidx[j],
                                               cache_b_kv[j], w_in_e[j], idx_k_gain[j], w_out_e[j])
            a_kv_p.append(akv); a_ix_p.append(aix); b_kv_p.append(bkv)
            a_kv_s.append(akv2); a_ix_s.append(aix2); b_kv_s.append(bkv2)
        else:
            yp, ckv = odd_prompt(rmsnorm(xp, norm_o[j]), w_in_o[j], rel_bias_o[j], w_out_o[j])
            ys, ckv2 = odd_sample(rmsnorm(xs, norm_o[j]), cache_c_kv[j], p_len, w_in_o[j], rel_bias_o[j], w_out_o[j])
            c_kv_p.append(ckv)
            c_kv_s.append(ckv2)
        xp = xp + yp
        xs = xs + ys
    y_prompt = rmsnorm(xp, norm_f)
    y_sample = rmsnorm(xs, norm_f)
    return (y_prompt, y_sample,
            jnp.stack(a_kv_p), jnp.stack(a_ix_p), jnp.stack(b_kv_p), jnp.stack(c_kv_p),
            jnp.stack(a_kv_s), jnp.stack(a_ix_s), jnp.stack(b_kv_s), jnp.stack(c_kv_s))
```

```python
import functools

import jax
import jax.numpy as jnp
from jax import lax
from jax.experimental import pallas as pl
from jax.experimental.pallas import tpu as pltpu

F32 = jnp.float32
BF16 = jnp.bfloat16

HEAD_DIM = 128
CHUNK = 64
CHUNK_SHIFT = 6
N_IDX_HEADS = 16
IDX_DIM = 64
TOPK_MAX = 256
C_PAST_CHUNKS = 8
REL_CLIP = 64
EPS = 1e-6
ATT_SCALE = HEAD_DIM ** -0.5
IDX_SCALE = (IDX_DIM ** -0.5) * (N_IDX_HEADS ** -0.5)

TQ = 128
LANE = 128
BAND_BLOCKS = (C_PAST_CHUNKS * CHUNK) // TQ + 1
INT_MIN = -2 ** 31
NEG_INF = float("-inf")
VMEM_LIMIT = 56 * 1024 * 1024


def _cparams(sem):
    return pltpu.CompilerParams(dimension_semantics=sem, vmem_limit_bytes=VMEM_LIMIT)


def _pick(n, candidates):
    for c in candidates:
        if n % c == 0:
            return c
    raise ValueError(f"no block size in {candidates} divides {n}")


def _rmsnorm_kernel(x_ref, g_ref, o_ref):
    x = x_ref[...]
    ms = jnp.mean(x * x, axis=-1, keepdims=True)
    o_ref[...] = (x * lax.rsqrt(ms + EPS) * g_ref[...]).astype(o_ref.dtype)


def _rmsnorm(x, g, out_dtype):
    m, d = x.shape
    tm = _pick(m, (256, 128))
    return pl.pallas_call(
        _rmsnorm_kernel,
        grid=(m // tm,),
        in_specs=[pl.BlockSpec((tm, d), lambda i: (i, 0)),
                  pl.BlockSpec((1, d), lambda i: (0, 0))],
        out_specs=pl.BlockSpec((tm, d), lambda i: (i, 0)),
        out_shape=jax.ShapeDtypeStruct((m, d), out_dtype),
        compiler_params=_cparams(("parallel",)),
        name="rmsnorm",
    )(x, g.reshape(1, d))


def _iknorm_kernel(x_ref, g_ref, o_ref):
    x = x_ref[:, :IDX_DIM]
    ms = jnp.mean(x * x, axis=-1, keepdims=True)
    o_ref[...] = x * lax.rsqrt(ms + EPS) * g_ref[...]


def _iknorm(idx32, gk, col_block):
    m = idx32.shape[0]
    tm = _pick(m, (1024, 512, 256, 128))
    return pl.pallas_call(
        _iknorm_kernel,
        grid=(m // tm,),
        in_specs=[pl.BlockSpec((tm, LANE), lambda i: (i, col_block)),
                  pl.BlockSpec((1, IDX_DIM), lambda i: (0, 0))],
        out_specs=pl.BlockSpec((tm, IDX_DIM), lambda i: (i, 0)),
        out_shape=jax.ShapeDtypeStruct((m, IDX_DIM), F32),
        compiler_params=_cparams(("parallel",)),
        name="iknorm",
    )(idx32, gk.reshape(1, IDX_DIM))


def _mm_kernel(*refs, n_pairs, has_res, n_out):
    acc = None
    for p in range(n_pairs):
        d = jnp.dot(refs[2 * p][...], refs[2 * p + 1][...], preferred_element_type=F32)
        acc = d if acc is None else acc + d
    pos = 2 * n_pairs
    if has_res:
        acc = acc + refs[pos][...]
        pos += 1
    for o_ref in refs[pos:pos + n_out]:
        o_ref[...] = acc.astype(o_ref.dtype)


def _mm(pairs, out_dtypes, res=None, tm=1024, tn=512):
    m = pairs[0][0].shape[0]
    n = pairs[0][1].shape[1]
    tm = min(tm, m)
    tn = min(tn, n)
    assert m % tm == 0 and n % tn == 0, (m, n, tm, tn)
    in_specs, args = [], []
    for a, w in pairs:
        k = a.shape[1]
        in_specs += [pl.BlockSpec((tm, k), lambda i, j: (i, 0)),
                     pl.BlockSpec((k, tn), lambda i, j: (0, j))]
        args += [a, w]
    if res is not None:
        in_specs.append(pl.BlockSpec((tm, tn), lambda i, j: (i, j)))
        args.append(res)
    outs = pl.pallas_call(
        functools.partial(_mm_kernel, n_pairs=len(pairs), has_res=res is not None,
                          n_out=len(out_dtypes)),
        grid=(m // tm, n // tn),
        in_specs=in_specs,
        out_specs=[pl.BlockSpec((tm, tn), lambda i, j: (i, j)) for _ in out_dtypes],
        out_shape=[jax.ShapeDtypeStruct((m, n), dt) for dt in out_dtypes],
        compiler_params=_cparams(("parallel", "arbitrary")),
        name="matmul",
    )(*args)
    return outs[0] if len(out_dtypes) == 1 else tuple(outs)


def _nt_dot(a, b):
    return lax.dot_general(a, b, (((1,), (1,)), ((), ())), preferred_element_type=F32)


def _dsa_mask_kernel(iq_ref, ik_ref, iwt_ref, o_ref, keys_ref, *, q0, kb, nkbt, sk_valid, topk):
    i = pl.program_id(1)
    nkb = jnp.minimum(nkbt, (q0 + (i + 1) * TQ + kb - 1) // kb)
    w = iwt_ref[...] * IDX_SCALE
    qchunk = (q0 + i * TQ + lax.broadcasted_iota(jnp.int32, (1, TQ), 1)) >> CHUNK_SHIFT

    def build(t, _):
        off = pl.multiple_of(t * kb, kb)
        acc = jnp.zeros((kb, TQ), F32)
        for h in range(N_IDX_HEADS):
            e, p = h % 2, h // 2
            ikh = ik_ref[pl.ds(off, kb), e * LANE:(e + 1) * LANE]
            s = _nt_dot(ikh, iq_ref[:, p * LANE:(p + 1) * LANE])
            acc = acc + jnp.maximum(s, 0.0) * w[h:h + 1, :]
        acc = acc + 0.0
        bits = lax.bitcast_convert_type(acc, jnp.int32)
        key = jnp.where(bits < 0, bits ^ 0x7FFFFFFF, bits)
        kpos = off + lax.broadcasted_iota(jnp.int32, (kb, TQ), 0)
        adm = ((kpos >> CHUNK_SHIFT) <= qchunk) & (kpos < sk_valid)
        keys_ref[pl.ds(off, kb), :] = jnp.where(adm, key, INT_MIN)
        return 0

    lax.fori_loop(0, nkb, build, 0)

    def count(pred):
        def body(t, acc8):
            off = pl.multiple_of(t * kb, kb)
            hit = pred(keys_ref[pl.ds(off, kb), :]).astype(jnp.int32)
            return acc8 + hit.reshape(kb // 8, 8, TQ).sum(axis=0)
        acc8 = lax.fori_loop(0, nkb, body, jnp.zeros((8, TQ), jnp.int32))
        return acc8.sum(axis=0, keepdims=True)

    thr = jnp.where(count(lambda kx: kx >= 0) >= topk, 0, INT_MIN).astype(jnp.int32)

    def bit_step(t, thr):
        cand = thr | (jnp.int32(1) << (30 - t))
        return jnp.where(count(lambda kx: kx >= cand) >= topk, cand, thr)

    thr = lax.fori_loop(0, 31, bit_step, thr)
    need = (topk - count(lambda kx: kx > thr)).astype(F32)

    tri = (lax.broadcasted_iota(jnp.int32, (kb, kb), 0)
           >= lax.broadcasted_iota(jnp.int32, (kb, kb), 1)).astype(BF16)

    def emit(t, seen):
        off = pl.multiple_of(t * kb, kb)
        kx = keys_ref[pl.ds(off, kb), :]
        eq = kx == thr
        rank = jnp.dot(tri, eq.astype(BF16), preferred_element_type=F32) + seen
        sel = ((kx > thr) | (eq & (rank <= need))) & (kx != INT_MIN)
        o_ref[0, 0, t] = jnp.where(sel, 0.0, NEG_INF).T.astype(o_ref.dtype)
        return rank[kb - 1:kb, :]

    lax.fori_loop(0, nkb, emit, jnp.zeros((1, TQ), F32))

    def fill(t, _):
        o_ref[0, 0, t] = jnp.full((TQ, kb), NEG_INF, o_ref.dtype)
        return 0

    lax.fori_loop(nkb, nkbt, fill, 0)


def _dsa_mask(idx16, ik2, iwt, *, b, sq, sk, q0, sk_valid, topk, kb):
    nqt, nkbt = sq // TQ, sk // kb
    n_iq = N_IDX_HEADS * IDX_DIM
    return pl.pallas_call(
        functools.partial(_dsa_mask_kernel, q0=q0, kb=kb, nkbt=nkbt, sk_valid=sk_valid, topk=topk),
        grid=(b, nqt),
        in_specs=[pl.BlockSpec((TQ, n_iq), lambda bi, i: (bi * nqt + i, 0)),
                  pl.BlockSpec((sk, 2 * LANE), lambda bi, i: (bi, 0)),
                  pl.BlockSpec((N_IDX_HEADS, TQ), lambda bi, i: (0, bi * nqt + i))],
        out_specs=pl.BlockSpec((1, 1, nkbt, TQ, kb), lambda bi, i: (bi, i, 0, 0, 0)),
        out_shape=jax.ShapeDtypeStruct((b, nqt, nkbt, TQ, kb), BF16),
        scratch_shapes=[pltpu.VMEM((sk, TQ), jnp.int32)],
        compiler_params=_cparams(("parallel", "arbitrary")),
        name="dsa_mask",
    )(idx16, ik2, iwt)


def _gated(o, g_ref, o_ref):
    g = g_ref[...]
    o_ref[...] = (o * (g * jax.nn.sigmoid(g))).astype(o_ref.dtype)


def _dsa_attn_kernel(q_ref, k_ref, v_ref, m_ref, sl_ref, g_ref, o_ref, *, q0, kb, nkbt):
    i = pl.program_id(2)
    nkb = jnp.minimum(nkbt, (q0 + (i + 1) * TQ + kb - 1) // kb)
    q = q_ref[...]
    slope = sl_ref[0][:, :1]
    qpos = q0 + i * TQ + lax.broadcasted_iota(jnp.int32, (TQ, kb), 0)
    col = lax.broadcasted_iota(jnp.int32, (TQ, kb), 1)

    def body(t, carry):
        m, l, acc = carry
        off = pl.multiple_of(t * kb, kb)
        s = _nt_dot(q, k_ref[pl.ds(off, kb), :]) * ATT_SCALE
        dist = jnp.abs(qpos - (off + col)).astype(F32)
        s = s - slope * dist + m_ref[0, 0, t].astype(F32)
        m_new = jnp.maximum(m, s.max(axis=1, keepdims=True))
        m_use = jnp.where(m_new == NEG_INF, 0.0, m_new)
        alpha = jnp.exp(m - m_use)
        p = jnp.exp(s - m_use)
        l = alpha * l + p.sum(axis=1, keepdims=True)
        acc = alpha * acc + jnp.dot(p.astype(BF16), v_ref[pl.ds(off, kb), :],
                                    preferred_element_type=F32)
        return m_new, l, acc

    init = (jnp.full((TQ, 1), NEG_INF, F32), jnp.zeros((TQ, 1), F32), jnp.zeros((TQ, HEAD_DIM), F32))
    _, l, acc = lax.fori_loop(0, nkb, body, init)
    _gated(acc / l, g_ref, o_ref)


def _stick_kernel(q_ref, k_ref, v_ref, g_ref, o_ref, *, q0, kb, nkbt):
    i = pl.program_id(2)
    nkb = jnp.minimum(nkbt, (q0 + (i + 1) * TQ + kb - 1) // kb)
    q = q_ref[...]
    later = (lax.broadcasted_iota(jnp.int32, (kb, kb), 0)
             > lax.broadcasted_iota(jnp.int32, (kb, kb), 1)).astype(BF16)
    qpos = q0 + i * TQ + lax.broadcasted_iota(jnp.int32, (TQ, kb), 0)
    col = lax.broadcasted_iota(jnp.int32, (TQ, kb), 1)

    def body(t, carry):
        run, acc = carry
        off = pl.multiple_of((nkb - 1 - t) * kb, kb)
        z = _nt_dot(q, k_ref[pl.ds(off, kb), :]) * ATT_SCALE
        causal = (off + col) < qpos
        sp = jnp.log1p(jnp.exp(-jnp.abs(z)))
        log_beta = jnp.minimum(z, 0.0) - sp
        log_keep = jnp.where(causal, jnp.minimum(-z, 0.0) - sp, 0.0)
        hi = log_keep.astype(BF16)
        r1 = log_keep - hi.astype(F32)
        mid = r1.astype(BF16)
        lo = (r1 - mid.astype(F32)).astype(BF16)
        suffix = (jnp.dot(hi, later, preferred_element_type=F32)
                  + jnp.dot(mid, later, preferred_element_type=F32)
                  + jnp.dot(lo, later, preferred_element_type=F32))
        w = jnp.where(causal, jnp.exp(log_beta + suffix + run), 0.0)
        acc = acc + jnp.dot(w.astype(BF16), v_ref[pl.ds(off, kb), :], preferred_element_type=F32)
        return run + log_keep.sum(axis=1, keepdims=True), acc

    init = (jnp.zeros((TQ, 1), F32), jnp.zeros((TQ, HEAD_DIM), F32))
    _, acc = lax.fori_loop(0, nkb, body, init)
    _gated(acc, g_ref, o_ref)


def _band_kernel(q_ref, k_ref, v_ref, tab_ref, cst_ref, g_ref, o_ref, *, q0, sk_valid):
    i = pl.program_id(2)
    jabs = q0 // TQ + i
    back = BAND_BLOCKS - 1
    shift = back - jnp.minimum(jabs, back)
    s0 = pl.multiple_of(jnp.maximum(jabs - back, 0) * TQ, TQ)
    width = BAND_BLOCKS * TQ
    s = _nt_dot(q_ref[...], k_ref[pl.ds(s0, width), :]) * ATT_SCALE
    far = cst_ref[0][:, :1]
    near_prev, near_diag = tab_ref[0, 0], tab_ref[0, 1]
    blocks = []
    for r in range(BAND_BLOCKS):
        rp = r + shift
        blocks.append(jnp.where(rp == back - 1, near_prev, jnp.where(rp == back, near_diag, far)))
    bias = jnp.concatenate(blocks, axis=1)
    qpos = q0 + i * TQ + lax.broadcasted_iota(jnp.int32, (TQ, width), 0)
    kpos = s0 + lax.broadcasted_iota(jnp.int32, (TQ, width), 1)
    qc, kc = qpos >> CHUNK_SHIFT, kpos >> CHUNK_SHIFT
    allowed = (kc <= qc) & (kc >= qc - C_PAST_CHUNKS) & (kpos < sk_valid)
    s = jnp.where(allowed, s + bias, NEG_INF)
    p = jnp.exp(s - s.max(axis=1, keepdims=True))
    l = p.sum(axis=1, keepdims=True)
    o = jnp.dot(p.astype(BF16), v_ref[pl.ds(s0, width), :], preferred_element_type=F32)
    _gated(o / l, g_ref, o_ref)


def _attention(kind, q, kv, gate, extra, *, b, sq, sk, nh, q0, sk_valid=None, kb=None):
    nqt = sq // TQ
    row = lambda bi, h, i: (bi * nqt + i, h)
    q_spec = pl.BlockSpec((TQ, HEAD_DIM), row)
    k_spec = pl.BlockSpec((sk, HEAD_DIM), lambda bi, h, i: (bi, h))
    v_spec = pl.BlockSpec((sk, HEAD_DIM), lambda bi, h, i: (bi, nh + h))
    head_vec = pl.BlockSpec((1, 1, LANE), lambda bi, h, i: (h, 0, 0))
    if kind == "dsa":
        mask, slopes = extra
        nkbt = sk // kb
        body = functools.partial(_dsa_attn_kernel, q0=q0, kb=kb, nkbt=nkbt)
        extra_specs = [pl.BlockSpec((1, 1, nkbt, TQ, kb), lambda bi, h, i: (bi, i, 0, 0, 0)), head_vec]
        extra_args = [mask, slopes]
    elif kind == "stick":
        body = functools.partial(_stick_kernel, q0=q0, kb=kb, nkbt=sk // kb)
        extra_specs, extra_args = [], []
    else:
        tab, cst = extra
        body = functools.partial(_band_kernel, q0=q0, sk_valid=sk_valid)
        extra_specs = [pl.BlockSpec((1, 2, TQ, TQ), lambda bi, h, i: (h, 0, 0, 0)), head_vec]
        extra_args = [tab, cst]
    return pl.pallas_call(
        body,
        grid=(b, nh, nqt),
        in_specs=[q_spec, k_spec, v_spec] + extra_specs + [q_spec],
        out_specs=q_spec,
        out_shape=jax.ShapeDtypeStruct((b * sq, nh * HEAD_DIM), BF16),
        compiler_params=_cparams(("parallel", "parallel", "arbitrary")),
        name=kind + "_attention",
    )(q, kv, kv, *extra_args, gate)


def _with_cache(cache, new, b, sq, n_new, lead):
    width = new.shape[-1]
    p = cache.shape[1]
    new = new.reshape(b, sq, width)[:, :n_new]
    tail = sq - n_new
    parts = [jnp.zeros((b, lead, width), BF16)] if lead else []
    parts += [cache.reshape(b, p, width).astype(BF16), new.astype(BF16), jnp.zeros((b, tail, width), BF16)]
    return jnp.concatenate(parts, axis=1).reshape(-1, width)


def _even_layer(x, wts, cache, *, b, sq, q0, n_new):
    na = wts["qa"].shape[1] // HEAD_DIM
    nb = wts["qb"].shape[1] // HEAD_DIM
    h = _rmsnorm(x, wts["norm"], BF16)
    qa = _mm([(h, wts["qa"])], [BF16])
    kva32, kva16 = _mm([(h, wts["kva"])], [F32, BF16])
    ga = _mm([(h, wts["ga"])], [F32])
    idx32, idx16 = _mm([(h, wts["idx"])], [F32, BF16], tm=512, tn=wts["idx"].shape[1])
    qb = _mm([(h, wts["qb"])], [BF16])
    kvb32, kvb16 = _mm([(h, wts["kvb"])], [F32, BF16])
    gb = _mm([(h, wts["gb"])], [F32])

    n_iq = N_IDX_HEADS * IDX_DIM
    ikn = _iknorm(idx32, wts["gk"], n_iq // LANE)
    iwt = idx32[:, n_iq + IDX_DIM:n_iq + IDX_DIM + N_IDX_HEADS].T

    if cache is None:
        sk = sk_valid = sq
        ka_all, kb_all = kva16, kvb16
        ik_all = ikn.reshape(b, sk, IDX_DIM).astype(BF16)
    else:
        a_kv, a_kidx, b_kv = cache
        sk, sk_valid = q0 + sq, q0 + n_new
        ka_all = _with_cache(a_kv, kva16, b, sq, n_new, 0)
        kb_all = _with_cache(b_kv, kvb16, b, sq, n_new, 0)
        ik_all = _with_cache(a_kidx, ikn, b, sq, n_new, 0).reshape(b, sk, IDX_DIM)
    zeros = jnp.zeros_like(ik_all)
    ik2 = jnp.concatenate([ik_all, zeros, zeros, ik_all], axis=-1).reshape(b * sk, 2 * LANE)

    kb_d = _pick(sk, (512, 384, 256, 128))
    mask = _dsa_mask(idx16, ik2, iwt, b=b, sq=sq, sk=sk, q0=q0, sk_valid=sk_valid,
                     topk=min(TOPK_MAX, sk_valid // 4), kb=kb_d)
    slopes = 2.0 ** (-8.0 * jnp.arange(1, na + 1, dtype=F32) / na)
    slopes = jnp.broadcast_to(slopes[:, None, None], (na, 1, LANE))
    oa = _attention("dsa", qa, ka_all, ga, (mask, slopes), b=b, sq=sq, sk=sk, nh=na, q0=q0, kb=kb_d)
    ob = _attention("stick", qb, kb_all, gb, None, b=b, sq=sq, sk=sk, nh=nb, q0=q0,
                    kb=_pick(sk, (256, 384, 128)))
    x_new = _mm([(oa, wts["out_a"]), (ob, wts["out_b"])], [F32], res=x)
    return x_new, kva32, ikn, kvb32


def _odd_layer(x, wts, cache, *, b, sq, q0, n_new):
    nc = wts["q"].shape[1] // HEAD_DIM
    h = _rmsnorm(x, wts["norm"], BF16)
    q = _mm([(h, wts["q"])], [BF16])
    kv32, kv16 = _mm([(h, wts["kv"])], [F32, BF16])
    g = _mm([(h, wts["g"])], [F32])
    if cache is None:
        sk = sk_valid = sq
        kv_all = kv16
    else:
        sk, sk_valid = q0 + sq, q0 + n_new
        kv_all = _with_cache(cache, kv16, b, sq, n_new, q0 - cache.shape[1])
    o = _attention("band", q, kv_all, g, (wts["tab"], wts["far"]), b=b, sq=sq, sk=sk, nh=nc,
                   q0=q0, sk_valid=sk_valid)
    x_new = _mm([(o, wts["out"])], [F32], res=x)
    return x_new, kv32


def _even_weights(norm, w_in, gk, w_out, d):
    wa = d // 2
    n_idx = N_IDX_HEADS * IDX_DIM + IDX_DIM + N_IDX_HEADS
    n_idx_pad = -(-n_idx // LANE) * LANE
    o_idx = 4 * wa
    o_b = o_idx + n_idx
    cut = lambda lo, n: w_in[:, lo:lo + n].astype(BF16)
    return {
        "norm": norm, "gk": gk,
        "qa": cut(0, wa), "kva": cut(wa, 2 * wa), "ga": cut(3 * wa, wa),
        "idx": jnp.pad(cut(o_idx, n_idx), ((0, 0), (0, n_idx_pad - n_idx))),
        "qb": cut(o_b, wa), "kvb": cut(o_b + wa, 2 * wa), "gb": cut(o_b + 3 * wa, wa),
        "out_a": w_out[:wa].astype(BF16), "out_b": w_out[wa:].astype(BF16),
    }


def _odd_weights(norm, w_in, rel_bias, w_out, d):
    t = jnp.arange(TQ)[:, None]
    u = jnp.arange(TQ)[None, :]
    rel_prev = jnp.clip(t + TQ - u, -REL_CLIP, REL_CLIP) + REL_CLIP
    rel_diag = jnp.clip(t - u, -REL_CLIP, REL_CLIP) + REL_CLIP
    nc = rel_bias.shape[0]
    return {
        "norm": norm,
        "q": w_in[:, :d].astype(BF16), "kv": w_in[:, d:3 * d].astype(BF16), "g": w_in[:, 3 * d:].astype(BF16),
        "tab": jnp.stack([rel_bias[:, rel_prev], rel_bias[:, rel_diag]], axis=1),
        "far": jnp.broadcast_to(rel_bias[:, 2 * REL_CLIP][:, None, None], (nc, 1, LANE)),
        "out": w_out.astype(BF16),
    }


def kernel(x_prompt, x_sample, cache_a_kv, cache_a_kidx, cache_b_kv, cache_c_kv, norm_e, w_in_e,
           idx_k_gain, w_out_e, norm_o, w_in_o, rel_bias_o, w_out_o, norm_f):
    bp, sp, d = x_prompt.shape
    bs, ts, _ = x_sample.shape
    p_len = cache_a_kv.shape[2]
    depth = norm_e.shape[0] + norm_o.shape[0]
    assert sp % TQ == 0 and p_len % TQ == 0 and ts <= TQ
    assert REL_CLIP <= CHUNK and TQ == 2 * CHUNK

    xp = x_prompt.reshape(bp * sp, d)
    xs = jnp.pad(x_sample, ((0, 0), (0, TQ - ts), (0, 0))).reshape(bs * TQ, d)
    rows_p = lambda a, *tail: a.reshape(bp, sp, *tail)
    rows_s = lambda a, *tail: a.reshape(bs, TQ, -1)[:, :ts].reshape(bs, ts, *tail)
    outs = {k: [] for k in ("a_kv_p", "a_ix_p", "b_kv_p", "c_kv_p", "a_kv_s", "a_ix_s", "b_kv_s", "c_kv_s")}
    for layer in range(depth):
        j = layer // 2
        if layer % 2 == 0:
            wts = _even_weights(norm_e[j], w_in_e[j], idx_k_gain[j], w_out_e[j], d)
            na = wts["qa"].shape[1] // HEAD_DIM
            nb = wts["qb"].shape[1] // HEAD_DIM
            xp, akv, aix, bkv = _even_layer(xp, wts, None, b=bp, sq=sp, q0=0, n_new=sp)
            xs, akv2, aix2, bkv2 = _even_layer(xs, wts, (cache_a_kv[j], cache_a_kidx[j], cache_b_kv[j]),
                                               b=bs, sq=TQ, q0=p_len, n_new=ts)
            outs["a_kv_p"].append(rows_p(akv, 2, na, HEAD_DIM))
            outs["a_ix_p"].append(rows_p(aix, IDX_DIM))
            outs["b_kv_p"].append(rows_p(bkv, 2, nb, HEAD_DIM))
            outs["a_kv_s"].append(rows_s(akv2, 2, na, HEAD_DIM))
            outs["a_ix_s"].append(rows_s(aix2, IDX_DIM))
            outs["b_kv_s"].append(rows_s(bkv2, 2, nb, HEAD_DIM))
        else:
            wts = _odd_weights(norm_o[j], w_in_o[j], rel_bias_o[j], w_out_o[j], d)
            nc = wts["q"].shape[1] // HEAD_DIM
            wc = min(C_PAST_CHUNKS * CHUNK, sp)
            xp, ckv = _odd_layer(xp, wts, None, b=bp, sq=sp, q0=0, n_new=sp)
            xs, ckv2 = _odd_layer(xs, wts, cache_c_kv[j], b=bs, sq=TQ, q0=p_len, n_new=ts)
            outs["c_kv_p"].append(rows_p(ckv, 2, nc, HEAD_DIM)[:, sp - wc:])
            outs["c_kv_s"].append(rows_s(ckv2, 2, nc, HEAD_DIM))
    y_prompt = _rmsnorm(xp, norm_f, F32).reshape(bp, sp, d)
    y_sample = _rmsnorm(xs, norm_f, F32).reshape(bs, TQ, d)[:, :ts]
    return (y_prompt, y_sample,
            jnp.stack(outs["a_kv_p"]), jnp.stack(outs["a_ix_p"]), jnp.stack(outs["b_kv_p"]), jnp.stack(outs["c_kv_p"]),
            jnp.stack(outs["a_kv_s"]), jnp.stack(outs["a_ix_s"]), jnp.stack(outs["b_kv_s"]), jnp.stack(outs["c_kv_s"]))
```

```python
import functools

import jax
import jax.numpy as jnp
from jax import lax
from jax.experimental import pallas as pl
from jax.experimental.pallas import tpu as pltpu

F32 = jnp.float32
BF16 = jnp.bfloat16

HEAD_DIM = 128
CHUNK = 64
CHUNK_SHIFT = 6
N_IDX_HEADS = 16
IDX_DIM = 64
TOPK_MAX = 256
C_PAST_CHUNKS = 8
REL_CLIP = 64
EPS = 1e-6
ATT_SCALE = HEAD_DIM ** -0.5
IDX_SCALE = (IDX_DIM ** -0.5) * (N_IDX_HEADS ** -0.5)

TQ = 128
LANE = 128
BAND_BLOCKS = (C_PAST_CHUNKS * CHUNK) // TQ + 1
INT_MIN = -2 ** 31
NEG_INF = float("-inf")
VMEM_LIMIT = 56 * 1024 * 1024


def _cparams(sem):
    return pltpu.CompilerParams(dimension_semantics=sem, vmem_limit_bytes=VMEM_LIMIT)


def _pick(n, candidates):
    for c in candidates:
        if n % c == 0:
            return c
    raise ValueError(f"no block size in {candidates} divides {n}")


def _rmsnorm_kernel(x_ref, g_ref, o_ref):
    x = x_ref[...]
    ms = jnp.mean(x * x, axis=-1, keepdims=True)
    o_ref[...] = (x * lax.rsqrt(ms + EPS) * g_ref[...]).astype(o_ref.dtype)


def _rmsnorm(x, g, out_dtype):
    m, d = x.shape
    tm = _pick(m, (256, 128, 64, 32, 16, 8))
    return pl.pallas_call(
        _rmsnorm_kernel,
        grid=(m // tm,),
        in_specs=[pl.BlockSpec((tm, d), lambda i: (i, 0)),
                  pl.BlockSpec((1, d), lambda i: (0, 0))],
        out_specs=pl.BlockSpec((tm, d), lambda i: (i, 0)),
        out_shape=jax.ShapeDtypeStruct((m, d), out_dtype),
        compiler_params=_cparams(("parallel",)),
        name="rmsnorm",
    )(x, g.reshape(1, d))


def _iknorm_kernel(x_ref, g_ref, o_ref):
    x = x_ref[:, :IDX_DIM]
    ms = jnp.mean(x * x, axis=-1, keepdims=True)
    o_ref[...] = x * lax.rsqrt(ms + EPS) * g_ref[...]


def _iknorm(idx32, gk, col_block):
    m = idx32.shape[0]
    tm = _pick(m, (1024, 512, 256, 128, 64, 32, 16, 8))
    return pl.pallas_call(
        _iknorm_kernel,
        grid=(m // tm,),
        in_specs=[pl.BlockSpec((tm, LANE), lambda i: (i, col_block)),
                  pl.BlockSpec((1, IDX_DIM), lambda i: (0, 0))],
        out_specs=pl.BlockSpec((tm, IDX_DIM), lambda i: (i, 0)),
        out_shape=jax.ShapeDtypeStruct((m, IDX_DIM), F32),
        compiler_params=_cparams(("parallel",)),
        name="iknorm",
    )(idx32, gk.reshape(1, IDX_DIM))


def _mm_kernel(*refs, n_pairs, has_res, n_out):
    acc = None
    for p in range(n_pairs):
        d = jnp.dot(refs[2 * p][...], refs[2 * p + 1][...], preferred_element_type=F32)
        acc = d if acc is None else acc + d
    pos = 2 * n_pairs
    if has_res:
        acc = acc + refs[pos][...]
        pos += 1
    for o_ref in refs[pos:pos + n_out]:
        o_ref[...] = acc.astype(o_ref.dtype)


def _mm(pairs, out_dtypes, res=None, tm=1024, tn=512):
    m = pairs[0][0].shape[0]
    n = pairs[0][1].shape[1]
    tm = min(tm, m)
    tn = min(tn, n)
    assert m % tm == 0 and n % tn == 0, (m, n, tm, tn)
    in_specs, args = [], []
    for a, w in pairs:
        k = a.shape[1]
        in_specs += [pl.BlockSpec((tm, k), lambda i, j: (i, 0)),
                     pl.BlockSpec((k, tn), lambda i, j: (0, j))]
        args += [a, w]
    if res is not None:
        in_specs.append(pl.BlockSpec((tm, tn), lambda i, j: (i, j)))
        args.append(res)
    outs = pl.pallas_call(
        functools.partial(_mm_kernel, n_pairs=len(pairs), has_res=res is not None,
                          n_out=len(out_dtypes)),
        grid=(m // tm, n // tn),
        in_specs=in_specs,
        out_specs=[pl.BlockSpec((tm, tn), lambda i, j: (i, j)) for _ in out_dtypes],
        out_shape=[jax.ShapeDtypeStruct((m, n), dt) for dt in out_dtypes],
        compiler_params=_cparams(("parallel", "arbitrary")),
        name="matmul",
    )(*args)
    return outs[0] if len(out_dtypes) == 1 else tuple(outs)


def _nt_dot(a, b):
    return lax.dot_general(a, b, (((1,), (1,)), ((), ())), preferred_element_type=F32)


def _dsa_mask_kernel(iq_ref, ik_ref, iwt_ref, o_ref, keys_ref, *, q0, kb, nkbt, sk_valid, topk):
    i = pl.program_id(1)
    nkb = jnp.minimum(nkbt, (q0 + (i + 1) * TQ + kb - 1) // kb)
    w = iwt_ref[...] * IDX_SCALE
    qchunk = (q0 + i * TQ + lax.broadcasted_iota(jnp.int32, (1, TQ), 1)) >> CHUNK_SHIFT

    def build(t, _):
        off = pl.multiple_of(t * kb, kb)
        acc = jnp.zeros((kb, TQ), F32)
        for h in range(N_IDX_HEADS):
            e, p = h % 2, h // 2
            ikh = ik_ref[pl.ds(off, kb), e * LANE:(e + 1) * LANE]
            s = _nt_dot(ikh, iq_ref[:, p * LANE:(p + 1) * LANE])
            acc = acc + jnp.maximum(s, 0.0) * w[h:h + 1, :]
        acc = acc + 0.0
        bits = lax.bitcast_convert_type(acc, jnp.int32)
        key = jnp.where(bits < 0, bits ^ 0x7FFFFFFF, bits)
        kpos = off + lax.broadcasted_iota(jnp.int32, (kb, TQ), 0)
        adm = ((kpos >> CHUNK_SHIFT) <= qchunk) & (kpos < sk_valid)
        keys_ref[pl.ds(off, kb), :] = jnp.where(adm, key, INT_MIN)
        return 0

    lax.fori_loop(0, nkb, build, 0)

    def count(pred):
        def body(t, acc8):
            off = pl.multiple_of(t * kb, kb)
            hit = pred(keys_ref[pl.ds(off, kb), :]).astype(jnp.int32)
            return acc8 + hit.reshape(kb // 8, 8, TQ).sum(axis=0)
        acc8 = lax.fori_loop(0, nkb, body, jnp.zeros((8, TQ), jnp.int32))
        return acc8.sum(axis=0, keepdims=True)

    thr = jnp.where(count(lambda kx: kx >= 0) >= topk, 0, INT_MIN).astype(jnp.int32)

    def bit_step(t, thr):
        cand = thr | (jnp.int32(1) << (30 - t))
        return jnp.where(count(lambda kx: kx >= cand) >= topk, cand, thr)

    thr = lax.fori_loop(0, 31, bit_step, thr)
    need = (topk - count(lambda kx: kx > thr)).astype(F32)

    tri = (lax.broadcasted_iota(jnp.int32, (kb, kb), 0)
           >= lax.broadcasted_iota(jnp.int32, (kb, kb), 1)).astype(BF16)

    def emit(t, seen):
        off = pl.multiple_of(t * kb, kb)
        kx = keys_ref[pl.ds(off, kb), :]
        eq = kx == thr
        rank = jnp.dot(tri, eq.astype(BF16), preferred_element_type=F32) + seen
        sel = ((kx > thr) | (eq & (rank <= need))) & (kx != INT_MIN)
        o_ref[0, 0, t] = jnp.where(sel, 0.0, NEG_INF).T.astype(o_ref.dtype)
        return rank[kb - 1:kb, :]

    lax.fori_loop(0, nkb, emit, jnp.zeros((1, TQ), F32))

    def fill(t, _):
        o_ref[0, 0, t] = jnp.full((TQ, kb), NEG_INF, o_ref.dtype)
        return 0

    lax.fori_loop(nkb, nkbt, fill, 0)


def _dsa_mask(idx16, ik2, iwt, *, b, sq, sk, q0, sk_valid, topk, kb):
    nqt, nkbt = sq // TQ, sk // kb
    n_iq = N_IDX_HEADS * IDX_DIM
    return pl.pallas_call(
        functools.partial(_dsa_mask_kernel, q0=q0, kb=kb, nkbt=nkbt, sk_valid=sk_valid, topk=topk),
        grid=(b, nqt),
        in_specs=[pl.BlockSpec((TQ, n_iq), lambda bi, i: (bi * nqt + i, 0)),
                  pl.BlockSpec((sk, 2 * LANE), lambda bi, i: (bi, 0)),
                  pl.BlockSpec((N_IDX_HEADS, TQ), lambda bi, i: (0, bi * nqt + i))],
        out_specs=pl.BlockSpec((1, 1, nkbt, TQ, kb), lambda bi, i: (bi, i, 0, 0, 0)),
        out_shape=jax.ShapeDtypeStruct((b, nqt, nkbt, TQ, kb), BF16),
        scratch_shapes=[pltpu.VMEM((sk, TQ), jnp.int32)],
        compiler_params=_cparams(("parallel", "arbitrary")),
        name="dsa_mask",
    )(idx16, ik2, iwt)


def _gated(o, g):
    return o * (g * jax.nn.sigmoid(g))


def _head(g):
    return slice(g * HEAD_DIM, (g + 1) * HEAD_DIM)


def _dsa_attn_kernel(q_ref, k_ref, v_ref, m_ref, sl_ref, g_ref, o_ref, *, q0, kb, nkbt, gh):
    i = pl.program_id(2)
    nkb = jnp.minimum(nkbt, (q0 + (i + 1) * TQ + kb - 1) // kb)
    qpos = q0 + i * TQ + lax.broadcasted_iota(jnp.int32, (TQ, kb), 0)
    col = lax.broadcasted_iota(jnp.int32, (TQ, kb), 1)

    def body(t, carry):
        off = pl.multiple_of(t * kb, kb)
        dist = jnp.abs(qpos - (off + col)).astype(F32)
        mask = m_ref[0, 0, t].astype(F32)
        qk = [_nt_dot(q_ref[:, _head(g)], k_ref[pl.ds(off, kb), _head(g)]) for g in range(gh)]
        soft = []
        for g in range(gh):
            m, l, _ = carry[g]
            s = qk[g] * ATT_SCALE - sl_ref[g][:, :1] * dist + mask
            m_new = jnp.maximum(m, s.max(axis=1, keepdims=True))
            m_use = jnp.where(m_new == NEG_INF, 0.0, m_new)
            alpha = jnp.exp(m - m_use)
            p = jnp.exp(s - m_use)
            soft.append((m_new, alpha * l + p.sum(axis=1, keepdims=True), alpha, p.astype(BF16)))
        pv = [jnp.dot(soft[g][3], v_ref[pl.ds(off, kb), _head(g)], preferred_element_type=F32)
              for g in range(gh)]
        return tuple((soft[g][0], soft[g][1], soft[g][2] * carry[g][2] + pv[g]) for g in range(gh))

    init = tuple((jnp.full((TQ, 1), NEG_INF, F32), jnp.zeros((TQ, 1), F32),
                  jnp.zeros((TQ, HEAD_DIM), F32)) for _ in range(gh))
    res = lax.fori_loop(0, nkb, body, init)
    for g in range(gh):
        _, l, acc = res[g]
        o_ref[:, _head(g)] = _gated(acc / l, g_ref[:, _head(g)]).astype(o_ref.dtype)


def _stick_kernel(q_ref, k_ref, v_ref, g_ref, o_ref, *, q0, kb, nkbt, gh):
    i = pl.program_id(2)
    nkb = jnp.minimum(nkbt, (q0 + (i + 1) * TQ + kb - 1) // kb)
    later = (lax.broadcasted_iota(jnp.int32, (kb, kb), 0)
             > lax.broadcasted_iota(jnp.int32, (kb, kb), 1)).astype(BF16)
    qpos = q0 + i * TQ + lax.broadcasted_iota(jnp.int32, (TQ, kb), 0)
    col = lax.broadcasted_iota(jnp.int32, (TQ, kb), 1)

    def body(t, carry):
        off = pl.multiple_of((nkb - 1 - t) * kb, kb)
        causal = (off + col) < qpos
        qk = [_nt_dot(q_ref[:, _head(g)], k_ref[pl.ds(off, kb), _head(g)]) for g in range(gh)]
        logs = []
        for g in range(gh):
            z = qk[g] * ATT_SCALE
            sp = jnp.log(1.0 + jnp.exp(-jnp.abs(z)))
            drop = jnp.where(causal, jnp.maximum(z, 0.0) + sp, 0.0)
            logs.append((jnp.minimum(z, 0.0) - sp - carry[g][0], drop))
        suffix = []
        for g in range(gh):
            drop = logs[g][1]
            hi = drop.astype(BF16)
            lo = (drop - hi.astype(F32)).astype(BF16)
            suffix.append(jnp.dot(hi, later, preferred_element_type=F32)
                          + jnp.dot(lo, later, preferred_element_type=F32))
        w = [jnp.where(causal, jnp.exp(logs[g][0] - suffix[g]), 0.0).astype(BF16) for g in range(gh)]
        pv = [jnp.dot(w[g], v_ref[pl.ds(off, kb), _head(g)], preferred_element_type=F32) for g in range(gh)]
        return tuple((carry[g][0] + suffix[g][:, :1] + logs[g][1][:, :1], carry[g][1] + pv[g])
                     for g in range(gh))

    init = tuple((jnp.zeros((TQ, 1), F32), jnp.zeros((TQ, HEAD_DIM), F32)) for _ in range(gh))
    res = lax.fori_loop(0, nkb, body, init)
    for g in range(gh):
        o_ref[:, _head(g)] = _gated(res[g][1], g_ref[:, _head(g)]).astype(o_ref.dtype)


def _band_kernel(q_ref, k_ref, v_ref, tab_ref, cst_ref, g_ref, o_ref, *, q0, sk_valid, gh):
    i = pl.program_id(2)
    jabs = q0 // TQ + i
    back = BAND_BLOCKS - 1
    shift = back - jnp.minimum(jabs, back)
    s0 = pl.multiple_of(jnp.maximum(jabs - back, 0) * TQ, TQ)
    width = BAND_BLOCKS * TQ
    qpos = q0 + i * TQ + lax.broadcasted_iota(jnp.int32, (TQ, width), 0)
    kpos = s0 + lax.broadcasted_iota(jnp.int32, (TQ, width), 1)
    qc, kc = qpos >> CHUNK_SHIFT, kpos >> CHUNK_SHIFT
    allowed = (kc <= qc) & (kc >= qc - C_PAST_CHUNKS) & (kpos < sk_valid)
    qk = [_nt_dot(q_ref[:, _head(g)], k_ref[pl.ds(s0, width), _head(g)]) for g in range(gh)]
    soft = []
    for g in range(gh):
        far = cst_ref[g][:, :1]
        near_prev, near_diag = tab_ref[g, 0], tab_ref[g, 1]
        blocks = []
        for r in range(BAND_BLOCKS):
            rp = r + shift
            blocks.append(jnp.where(rp == back - 1, near_prev, jnp.where(rp == back, near_diag, far)))
        s = jnp.where(allowed, qk[g] * ATT_SCALE + jnp.concatenate(blocks, axis=1), NEG_INF)
        p = jnp.exp(s - s.max(axis=1, keepdims=True))
        soft.append((p.sum(axis=1, keepdims=True), p.astype(BF16)))
    pv = [jnp.dot(soft[g][1], v_ref[pl.ds(s0, width), _head(g)], preferred_element_type=F32)
          for g in range(gh)]
    for g in range(gh):
        o_ref[:, _head(g)] = _gated(pv[g] / soft[g][0], g_ref[:, _head(g)]).astype(o_ref.dtype)


def _attention(kind, q, kv, gate, extra, *, b, sq, sk, nh, q0, sk_valid=None, kb=None, gh=8):
    nqt = sq // TQ
    assert nh % gh == 0
    ng = nh // gh
    gw = gh * HEAD_DIM
    q_spec = pl.BlockSpec((TQ, gw), lambda bi, h, i: (bi * nqt + i, h))
    k_spec = pl.BlockSpec((sk, gw), lambda bi, h, i: (bi, h))
    v_spec = pl.BlockSpec((sk, gw), lambda bi, h, i: (bi, ng + h))
    head_vec = pl.BlockSpec((gh, 1, LANE), lambda bi, h, i: (h, 0, 0))
    if kind == "dsa":
        mask, slopes = extra
        nkbt = sk // kb
        body = functools.partial(_dsa_attn_kernel, q0=q0, kb=kb, nkbt=nkbt, gh=gh)
        extra_specs = [pl.BlockSpec((1, 1, nkbt, TQ, kb), lambda bi, h, i: (bi, i, 0, 0, 0)), head_vec]
        extra_args = [mask, slopes]
    elif kind == "stick":
        body = functools.partial(_stick_kernel, q0=q0, kb=kb, nkbt=sk // kb, gh=gh)
        extra_specs, extra_args = [], []
    else:
        tab, cst = extra
        body = functools.partial(_band_kernel, q0=q0, sk_valid=sk_valid, gh=gh)
        extra_specs = [pl.BlockSpec((gh, 2, TQ, TQ), lambda bi, h, i: (h, 0, 0, 0)), head_vec]
        extra_args = [tab, cst]
    return pl.pallas_call(
        body,
        grid=(b, ng, nqt),
        in_specs=[q_spec, k_spec, v_spec] + extra_specs + [q_spec],
        out_specs=q_spec,
        out_shape=jax.ShapeDtypeStruct((b * sq, nh * HEAD_DIM), BF16),
        compiler_params=_cparams(("parallel", "parallel", "arbitrary")),
        name=kind + "_attention",
    )(q, kv, kv, *extra_args, gate)


def _pad_q(a, b, nq, sq):
    if nq == sq:
        return a
    return jnp.pad(a.reshape(b, nq, -1), ((0, 0), (0, sq - nq), (0, 0))).reshape(b * sq, -1)


def _unpad_q(a, b, nq, sq):
    if nq == sq:
        return a
    return a.reshape(b, sq, -1)[:, :nq].reshape(b * nq, -1)


def _with_cache(cache, new, b, nq, sq, lead):
    width = new.shape[-1]
    parts = [jnp.zeros((b, lead, width), BF16)] if lead else []
    parts += [cache.reshape(b, cache.shape[1], width).astype(BF16), new.reshape(b, nq, width).astype(BF16),
              jnp.zeros((b, sq - nq, width), BF16)]
    return jnp.concatenate(parts, axis=1).reshape(-1, width)


def _even_layer(x, wts, cache, *, b, nq, q0):
    sq = -(-nq // TQ) * TQ
    na = wts["qa"].shape[1] // HEAD_DIM
    nb = wts["qb"].shape[1] // HEAD_DIM
    h = _rmsnorm(x, wts["norm"], BF16)
    qa = _mm([(h, wts["qa"])], [BF16])
    kva32, kva16 = _mm([(h, wts["kva"])], [F32, BF16])
    ga = _mm([(h, wts["ga"])], [F32])
    idx32, idx16 = _mm([(h, wts["idx"])], [F32, BF16], tm=512, tn=wts["idx"].shape[1])
    qb = _mm([(h, wts["qb"])], [BF16])
    kvb32, kvb16 = _mm([(h, wts["kvb"])], [F32, BF16])
    gb = _mm([(h, wts["gb"])], [F32])

    n_iq = N_IDX_HEADS * IDX_DIM
    ikn = _iknorm(idx32, wts["gk"], n_iq // LANE)
    iw = idx32[:, n_iq + IDX_DIM:n_iq + IDX_DIM + N_IDX_HEADS]
    iwt = _pad_q(iw, b, nq, sq).T

    if cache is None:
        sk = sk_valid = sq
        ka_all, kb_all = kva16, kvb16
        ik_all = ikn.reshape(b, sk, IDX_DIM).astype(BF16)
    else:
        a_kv, a_kidx, b_kv = cache
        sk, sk_valid = q0 + sq, q0 + nq
        ka_all = _with_cache(a_kv, kva16, b, nq, sq, 0)
        kb_all = _with_cache(b_kv, kvb16, b, nq, sq, 0)
        ik_all = _with_cache(a_kidx, ikn, b, nq, sq, 0).reshape(b, sk, IDX_DIM)
    zeros = jnp.zeros_like(ik_all)
    ik2 = jnp.concatenate([ik_all, zeros, zeros, ik_all], axis=-1).reshape(b * sk, 2 * LANE)

    pad = functools.partial(_pad_q, b=b, nq=nq, sq=sq)
    kb_d = _pick(sk, (512, 384, 256, 128))
    mask = _dsa_mask(pad(idx16), ik2, iwt, b=b, sq=sq, sk=sk, q0=q0, sk_valid=sk_valid,
                     topk=min(TOPK_MAX, sk_valid // 4), kb=kb_d)
    slopes = 2.0 ** (-8.0 * jnp.arange(1, na + 1, dtype=F32) / na)
    slopes = jnp.broadcast_to(slopes[:, None, None], (na, 1, LANE))
    oa = _attention("dsa", pad(qa), ka_all, pad(ga), (mask, slopes), b=b, sq=sq, sk=sk, nh=na, q0=q0, kb=kb_d,
                    gh=4)
    ob = _attention("stick", pad(qb), kb_all, pad(gb), None, b=b, sq=sq, sk=sk, nh=nb, q0=q0,
                    kb=_pick(sk, (256, 384, 128)))
    x_new = _mm([(_unpad_q(oa, b, nq, sq), wts["out_a"]), (_unpad_q(ob, b, nq, sq), wts["out_b"])], [F32], res=x)
    return x_new, kva32, ikn, kvb32


def _odd_layer(x, wts, cache, *, b, nq, q0):
    sq = -(-nq // TQ) * TQ
    nc = wts["q"].shape[1] // HEAD_DIM
    h = _rmsnorm(x, wts["norm"], BF16)
    q = _mm([(h, wts["q"])], [BF16])
    kv32, kv16 = _mm([(h, wts["kv"])], [F32, BF16])
    g = _mm([(h, wts["g"])], [F32])
    if cache is None:
        sk = sk_valid = sq
        kv_all = kv16
    else:
        sk, sk_valid = q0 + sq, q0 + nq
        kv_all = _with_cache(cache, kv16, b, nq, sq, q0 - cache.shape[1])
    o = _attention("band", _pad_q(q, b, nq, sq), kv_all, _pad_q(g, b, nq, sq), (wts["tab"], wts["far"]),
                   b=b, sq=sq, sk=sk, nh=nc, q0=q0, sk_valid=sk_valid)
    x_new = _mm([(_unpad_q(o, b, nq, sq), wts["out"])], [F32], res=x)
    return x_new, kv32


def _even_weights(norm, w_in, gk, w_out, d):
    wa = d // 2
    n_idx = N_IDX_HEADS * IDX_DIM + IDX_DIM + N_IDX_HEADS
    n_idx_pad = -(-n_idx // LANE) * LANE
    o_idx = 4 * wa
    o_b = o_idx + n_idx
    cut = lambda lo, n: w_in[:, lo:lo + n].astype(BF16)
    return {
        "norm": norm, "gk": gk,
        "qa": cut(0, wa), "kva": cut(wa, 2 * wa), "ga": cut(3 * wa, wa),
        "idx": jnp.pad(cut(o_idx, n_idx), ((0, 0), (0, n_idx_pad - n_idx))),
        "qb": cut(o_b, wa), "kvb": cut(o_b + wa, 2 * wa), "gb": cut(o_b + 3 * wa, wa),
        "out_a": w_out[:wa].astype(BF16), "out_b": w_out[wa:].astype(BF16),
    }


def _odd_weights(norm, w_in, rel_bias, w_out, d):
    t = jnp.arange(TQ)[:, None]
    u = jnp.arange(TQ)[None, :]
    rel_prev = jnp.clip(t + TQ - u, -REL_CLIP, REL_CLIP) + REL_CLIP
    rel_diag = jnp.clip(t - u, -REL_CLIP, REL_CLIP) + REL_CLIP
    nc = rel_bias.shape[0]
    return {
        "norm": norm,
        "q": w_in[:, :d].astype(BF16), "kv": w_in[:, d:3 * d].astype(BF16), "g": w_in[:, 3 * d:].astype(BF16),
        "tab": jnp.stack([rel_bias[:, rel_prev], rel_bias[:, rel_diag]], axis=1),
        "far": jnp.broadcast_to(rel_bias[:, 2 * REL_CLIP][:, None, None], (nc, 1, LANE)),
        "out": w_out.astype(BF16),
    }


def kernel(x_prompt, x_sample, cache_a_kv, cache_a_kidx, cache_b_kv, cache_c_kv, norm_e, w_in_e,
           idx_k_gain, w_out_e, norm_o, w_in_o, rel_bias_o, w_out_o, norm_f):
    bp, sp, d = x_prompt.shape
    bs, ts, _ = x_sample.shape
    p_len = cache_a_kv.shape[2]
    depth = norm_e.shape[0] + norm_o.shape[0]
    assert sp % TQ == 0 and p_len % TQ == 0 and ts <= TQ
    assert REL_CLIP <= CHUNK and TQ == 2 * CHUNK

    xp = x_prompt.reshape(bp * sp, d)
    xs = x_sample.reshape(bs * ts, d)
    rows_p = lambda a, *tail: a.reshape(bp, sp, *tail)
    rows_s = lambda a, *tail: a.reshape(bs, ts, *tail)
    outs = {k: [] for k in ("a_kv_p", "a_ix_p", "b_kv_p", "c_kv_p", "a_kv_s", "a_ix_s", "b_kv_s", "c_kv_s")}
    for layer in range(depth):
        j = layer // 2
        if layer % 2 == 0:
            wts = _even_weights(norm_e[j], w_in_e[j], idx_k_gain[j], w_out_e[j], d)
            na = wts["qa"].shape[1] // HEAD_DIM
            nb = wts["qb"].shape[1] // HEAD_DIM
            xp, akv, aix, bkv = _even_layer(xp, wts, None, b=bp, nq=sp, q0=0)
            xs, akv2, aix2, bkv2 = _even_layer(xs, wts, (cache_a_kv[j], cache_a_kidx[j], cache_b_kv[j]),
                                               b=bs, nq=ts, q0=p_len)
            outs["a_kv_p"].append(rows_p(akv, 2, na, HEAD_DIM))
            outs["a_ix_p"].append(rows_p(aix, IDX_DIM))
            outs["b_kv_p"].append(rows_p(bkv, 2, nb, HEAD_DIM))
            outs["a_kv_s"].append(rows_s(akv2, 2, na, HEAD_DIM))
            outs["a_ix_s"].append(rows_s(aix2, IDX_DIM))
            outs["b_kv_s"].append(rows_s(bkv2, 2, nb, HEAD_DIM))
        else:
            wts = _odd_weights(norm_o[j], w_in_o[j], rel_bias_o[j], w_out_o[j], d)
            nc = wts["q"].shape[1] // HEAD_DIM
            wc = min(C_PAST_CHUNKS * CHUNK, sp)
            xp, ckv = _odd_layer(xp, wts, None, b=bp, nq=sp, q0=0)
            xs, ckv2 = _odd_layer(xs, wts, cache_c_kv[j], b=bs, nq=ts, q0=p_len)
            outs["c_kv_p"].append(rows_p(ckv, 2, nc, HEAD_DIM)[:, sp - wc:])
            outs["c_kv_s"].append(rows_s(ckv2, 2, nc, HEAD_DIM))
    y_prompt = _rmsnorm(xp, norm_f, F32).reshape(bp, sp, d)
    y_sample = _rmsnorm(xs, norm_f, F32).reshape(bs, ts, d)
    return (y_prompt, y_sample,
            jnp.stack(outs["a_kv_p"]), jnp.stack(outs["a_ix_p"]), jnp.stack(outs["b_kv_p"]), jnp.stack(outs["c_kv_p"]),
            jnp.stack(outs["a_kv_s"]), jnp.stack(outs["a_ix_s"]), jnp.stack(outs["b_kv_s"]), jnp.stack(outs["c_kv_s"]))
```

```python
import functools

import jax
import jax.numpy as jnp
from jax import lax
from jax.experimental import pallas as pl
from jax.experimental.pallas import tpu as pltpu

F32 = jnp.float32
BF16 = jnp.bfloat16

HEAD_DIM = 128
CHUNK = 64
CHUNK_SHIFT = 6
N_IDX_HEADS = 16
IDX_DIM = 64
TOPK_MAX = 256
C_PAST_CHUNKS = 8
REL_CLIP = 64
EPS = 1e-6
ATT_SCALE = HEAD_DIM ** -0.5
LOG2E = 1.4426950408889634
IDX_SCALE = (IDX_DIM ** -0.5) * (N_IDX_HEADS ** -0.5)

TQ = 128
LANE = 128
BAND_BLOCKS = (C_PAST_CHUNKS * CHUNK) // TQ + 1
INT_MIN = -2 ** 31
NEG_INF = float("-inf")
VMEM_LIMIT = 56 * 1024 * 1024


def _cparams(sem):
    return pltpu.CompilerParams(dimension_semantics=sem, vmem_limit_bytes=VMEM_LIMIT)


def _pick(n, candidates):
    for c in candidates:
        if n % c == 0:
            return c
    raise ValueError(f"no block size in {candidates} divides {n}")


def _rmsnorm_kernel(x_ref, g_ref, o_ref):
    x = x_ref[...]
    ms = jnp.mean(x * x, axis=-1, keepdims=True)
    o_ref[...] = (x * lax.rsqrt(ms + EPS) * g_ref[...]).astype(o_ref.dtype)


def _rmsnorm(x, g, out_dtype):
    m, d = x.shape
    tm = _pick(m, (256, 128, 64, 32, 16, 8))
    return pl.pallas_call(
        _rmsnorm_kernel,
        grid=(m // tm,),
        in_specs=[pl.BlockSpec((tm, d), lambda i: (i, 0)),
                  pl.BlockSpec((1, d), lambda i: (0, 0))],
        out_specs=pl.BlockSpec((tm, d), lambda i: (i, 0)),
        out_shape=jax.ShapeDtypeStruct((m, d), out_dtype),
        compiler_params=_cparams(("parallel",)),
        name="rmsnorm",
    )(x, g.reshape(1, d))


def _iknorm_kernel(x_ref, g_ref, o_ref):
    x = x_ref[:, :IDX_DIM]
    ms = jnp.mean(x * x, axis=-1, keepdims=True)
    o_ref[...] = x * lax.rsqrt(ms + EPS) * g_ref[...]


def _iknorm(idx32, gk, col_block):
    m = idx32.shape[0]
    tm = _pick(m, (1024, 512, 256, 128, 64, 32, 16, 8))
    return pl.pallas_call(
        _iknorm_kernel,
        grid=(m // tm,),
        in_specs=[pl.BlockSpec((tm, LANE), lambda i: (i, col_block)),
                  pl.BlockSpec((1, IDX_DIM), lambda i: (0, 0))],
        out_specs=pl.BlockSpec((tm, IDX_DIM), lambda i: (i, 0)),
        out_shape=jax.ShapeDtypeStruct((m, IDX_DIM), F32),
        compiler_params=_cparams(("parallel",)),
        name="iknorm",
    )(idx32, gk.reshape(1, IDX_DIM))


def _mm_kernel(*refs, n_pairs, has_res, n_out):
    acc = None
    for p in range(n_pairs):
        d = jnp.dot(refs[2 * p][...], refs[2 * p + 1][...], preferred_element_type=F32)
        acc = d if acc is None else acc + d
    pos = 2 * n_pairs
    if has_res:
        acc = acc + refs[pos][...]
        pos += 1
    for o_ref in refs[pos:pos + n_out]:
        o_ref[...] = acc.astype(o_ref.dtype)


def _mm(pairs, out_dtypes, res=None, tm=1024, tn=512):
    m = pairs[0][0].shape[0]
    n = pairs[0][1].shape[1]
    tm = min(tm, m)
    tn = min(tn, n)
    assert m % tm == 0 and n % tn == 0, (m, n, tm, tn)
    in_specs, args = [], []
    for a, w in pairs:
        k = a.shape[1]
        in_specs += [pl.BlockSpec((tm, k), lambda i, j: (i, 0)),
                     pl.BlockSpec((k, tn), lambda i, j: (0, j))]
        args += [a, w]
    if res is not None:
        in_specs.append(pl.BlockSpec((tm, tn), lambda i, j: (i, j)))
        args.append(res)
    outs = pl.pallas_call(
        functools.partial(_mm_kernel, n_pairs=len(pairs), has_res=res is not None,
                          n_out=len(out_dtypes)),
        grid=(m // tm, n // tn),
        in_specs=in_specs,
        out_specs=[pl.BlockSpec((tm, tn), lambda i, j: (i, j)) for _ in out_dtypes],
        out_shape=[jax.ShapeDtypeStruct((m, n), dt) for dt in out_dtypes],
        compiler_params=_cparams(("parallel", "arbitrary")),
        name="matmul",
    )(*args)
    return outs[0] if len(out_dtypes) == 1 else tuple(outs)


def _nt_dot(a, b):
    return lax.dot_general(a, b, (((1,), (1,)), ((), ())), preferred_element_type=F32)


def _dsa_mask_kernel(iq_ref, ik_ref, iwt_ref, o_ref, keys_ref, *, q0, kb, nkbt, sk_valid, topk):
    i = pl.program_id(1)
    nkb = jnp.minimum(nkbt, (q0 + (i + 1) * TQ + kb - 1) // kb)
    w = iwt_ref[...] * IDX_SCALE
    qchunk = (q0 + i * TQ + lax.broadcasted_iota(jnp.int32, (1, TQ), 1)) >> CHUNK_SHIFT

    def build(t, _):
        off = pl.multiple_of(t * kb, kb)
        acc = jnp.zeros((kb, TQ), F32)
        for h in range(N_IDX_HEADS):
            e, p = h % 2, h // 2
            ikh = ik_ref[pl.ds(off, kb), e * LANE:(e + 1) * LANE]
            s = _nt_dot(ikh, iq_ref[:, p * LANE:(p + 1) * LANE])
            acc = acc + jnp.maximum(s, 0.0) * w[h:h + 1, :]
        acc = acc + 0.0
        bits = lax.bitcast_convert_type(acc, jnp.int32)
        key = jnp.where(bits < 0, bits ^ 0x7FFFFFFF, bits)
        kpos = off + lax.broadcasted_iota(jnp.int32, (kb, TQ), 0)
        adm = ((kpos >> CHUNK_SHIFT) <= qchunk) & (kpos < sk_valid)
        keys_ref[pl.ds(off, kb), :] = jnp.where(adm, key, INT_MIN)
        return 0

    lax.fori_loop(0, nkb, build, 0)

    def count(pred):
        def body(t, acc8):
            off = pl.multiple_of(t * kb, kb)
            hit = pred(keys_ref[pl.ds(off, kb), :]).astype(jnp.int32)
            return acc8 + hit.reshape(kb // 8, 8, TQ).sum(axis=0)
        acc8 = lax.fori_loop(0, nkb, body, jnp.zeros((8, TQ), jnp.int32))
        return acc8.sum(axis=0, keepdims=True)

    thr = jnp.where(count(lambda kx: kx >= 0) >= topk, 0, INT_MIN).astype(jnp.int32)

    def bit_step(t, thr):
        cand = thr | (jnp.int32(1) << (30 - t))
        return jnp.where(count(lambda kx: kx >= cand) >= topk, cand, thr)

    thr = lax.fori_loop(0, 31, bit_step, thr)
    need = (topk - count(lambda kx: kx > thr)).astype(F32)

    tri = (lax.broadcasted_iota(jnp.int32, (kb, kb), 0)
           >= lax.broadcasted_iota(jnp.int32, (kb, kb), 1)).astype(BF16)

    def emit(t, seen):
        off = pl.multiple_of(t * kb, kb)
        kx = keys_ref[pl.ds(off, kb), :]
        eq = kx == thr
        rank = jnp.dot(tri, eq.astype(BF16), preferred_element_type=F32) + seen
        sel = ((kx > thr) | (eq & (rank <= need))) & (kx != INT_MIN)
        o_ref[0, 0, t] = jnp.where(sel, 0.0, NEG_INF).T.astype(o_ref.dtype)
        return rank[kb - 1:kb, :]

    lax.fori_loop(0, nkb, emit, jnp.zeros((1, TQ), F32))

    def fill(t, _):
        o_ref[0, 0, t] = jnp.full((TQ, kb), NEG_INF, o_ref.dtype)
        return 0

    lax.fori_loop(nkb, nkbt, fill, 0)


def _dsa_mask(idx16, ik2, iwt, *, b, sq, sk, q0, sk_valid, topk, kb):
    nqt, nkbt = sq // TQ, sk // kb
    n_iq = N_IDX_HEADS * IDX_DIM
    return pl.pallas_call(
        functools.partial(_dsa_mask_kernel, q0=q0, kb=kb, nkbt=nkbt, sk_valid=sk_valid, topk=topk),
        grid=(b, nqt),
        in_specs=[pl.BlockSpec((TQ, n_iq), lambda bi, i: (bi * nqt + i, 0)),
                  pl.BlockSpec((sk, 2 * LANE), lambda bi, i: (bi, 0)),
                  pl.BlockSpec((N_IDX_HEADS, TQ), lambda bi, i: (0, bi * nqt + i))],
        out_specs=pl.BlockSpec((1, 1, nkbt, TQ, kb), lambda bi, i: (bi, i, 0, 0, 0)),
        out_shape=jax.ShapeDtypeStruct((b, nqt, nkbt, TQ, kb), BF16),
        scratch_shapes=[pltpu.VMEM((sk, TQ), jnp.int32)],
        compiler_params=_cparams(("parallel", "arbitrary")),
        name="dsa_mask",
    )(idx16, ik2, iwt)


def _gated(o, g):
    return o * (g * jax.nn.sigmoid(g))


def _head(g):
    return slice(g * HEAD_DIM, (g + 1) * HEAD_DIM)


def _dsa_attn_kernel(q_ref, k_ref, v_ref, m_ref, sl_ref, g_ref, o_ref, *, q0, kb, nkbt, gh):
    i = pl.program_id(2)
    nkb = jnp.minimum(nkbt, (q0 + (i + 1) * TQ + kb - 1) // kb)
    qpos = q0 + i * TQ + lax.broadcasted_iota(jnp.int32, (TQ, kb), 0)
    col = lax.broadcasted_iota(jnp.int32, (TQ, kb), 1)

    def body(t, carry):
        off = pl.multiple_of(t * kb, kb)
        dist = jnp.abs(qpos - (off + col)).astype(F32)
        mask = m_ref[0, 0, t].astype(F32)
        qk = [_nt_dot(q_ref[:, _head(g)], k_ref[pl.ds(off, kb), _head(g)]) for g in range(gh)]
        soft = []
        for g in range(gh):
            m, l, _ = carry[g]
            s = qk[g] * (ATT_SCALE * LOG2E) - (sl_ref[g][:, :1] * LOG2E) * dist + mask
            m_new = jnp.maximum(m, s.max(axis=1, keepdims=True))
            m_use = jnp.where(m_new == NEG_INF, 0.0, m_new)
            alpha = jnp.exp2(m - m_use)
            p = jnp.exp2(s - m_use)
            soft.append((m_new, alpha * l + p.sum(axis=1, keepdims=True), alpha, p.astype(BF16)))
        pv = [jnp.dot(soft[g][3], v_ref[pl.ds(off, kb), _head(g)], preferred_element_type=F32)
              for g in range(gh)]
        return tuple((soft[g][0], soft[g][1], soft[g][2] * carry[g][2] + pv[g]) for g in range(gh))

    init = tuple((jnp.full((TQ, 1), NEG_INF, F32), jnp.zeros((TQ, 1), F32),
                  jnp.zeros((TQ, HEAD_DIM), F32)) for _ in range(gh))
    res = lax.fori_loop(0, nkb, body, init)
    for g in range(gh):
        _, l, acc = res[g]
        o_ref[:, _head(g)] = _gated(acc / l, g_ref[:, _head(g)]).astype(o_ref.dtype)


def _stick_kernel(q_ref, k_ref, v_ref, g_ref, o_ref, *, q0, kb, nkbt, gh):
    i = pl.program_id(2)
    nkb = jnp.minimum(nkbt, (q0 + (i + 1) * TQ + kb - 1) // kb)
    later = (lax.broadcasted_iota(jnp.int32, (kb, kb), 0)
             > lax.broadcasted_iota(jnp.int32, (kb, kb), 1)).astype(BF16)
    qpos = q0 + i * TQ + lax.broadcasted_iota(jnp.int32, (TQ, kb), 0)
    col = lax.broadcasted_iota(jnp.int32, (TQ, kb), 1)

    def block(off, carry, diagonal):
        causal = (off + col) < qpos
        qk = [_nt_dot(q_ref[:, _head(g)], k_ref[pl.ds(off, kb), _head(g)]) for g in range(gh)]
        logs = []
        for g in range(gh):
            z2 = qk[g] * (ATT_SCALE * LOG2E)
            sp = jnp.log2(1.0 + jnp.exp2(-jnp.abs(z2)))
            drop = jnp.maximum(z2, 0.0) + sp
            if diagonal:
                drop = jnp.where(causal, drop, 0.0)
            logs.append((jnp.minimum(z2, 0.0) - sp - carry[g][0], drop))
        suffix = []
        for g in range(gh):
            drop = logs[g][1]
            hi = drop.astype(BF16)
            lo = (drop - hi.astype(F32)).astype(BF16)
            suffix.append(jnp.dot(hi, later, preferred_element_type=F32)
                          + jnp.dot(lo, later, preferred_element_type=F32))
        w = []
        for g in range(gh):
            wg = jnp.exp2(logs[g][0] - suffix[g])
            if diagonal:
                wg = jnp.where(causal, wg, 0.0)
            w.append(wg.astype(BF16))
        pv = [jnp.dot(w[g], v_ref[pl.ds(off, kb), _head(g)], preferred_element_type=F32) for g in range(gh)]
        return tuple((carry[g][0] + suffix[g][:, :1] + logs[g][1][:, :1], carry[g][1] + pv[g])
                     for g in range(gh))

    init = tuple((jnp.zeros((TQ, 1), F32), jnp.zeros((TQ, HEAD_DIM), F32)) for _ in range(gh))
    first = block(pl.multiple_of((nkb - 1) * kb, kb), init, True)
    res = lax.fori_loop(1, nkb, lambda t, c: block(pl.multiple_of((nkb - 1 - t) * kb, kb), c, False), first)
    for g in range(gh):
        o_ref[:, _head(g)] = _gated(res[g][1], g_ref[:, _head(g)]).astype(o_ref.dtype)


def _band_kernel(q_ref, k_ref, v_ref, tab_ref, g_ref, o_ref, *, q0, sk_valid, gh):
    i = pl.program_id(2)
    jabs = q0 // TQ + i
    back = BAND_BLOCKS - 1
    shift = back - jnp.minimum(jabs, back)
    s0 = pl.multiple_of(jnp.maximum(jabs - back, 0) * TQ, TQ)
    width = BAND_BLOCKS * TQ
    qpos = q0 + i * TQ + lax.broadcasted_iota(jnp.int32, (TQ, width), 0)
    kpos = s0 + lax.broadcasted_iota(jnp.int32, (TQ, width), 1)
    qc, kc = qpos >> CHUNK_SHIFT, kpos >> CHUNK_SHIFT
    allowed = (kc <= qc) & (kc >= qc - C_PAST_CHUNKS) & (kpos < sk_valid)
    qk = [_nt_dot(q_ref[:, _head(g)], k_ref[pl.ds(s0, width), _head(g)]) for g in range(gh)]
    entry = [jnp.clip(r + shift - (back - 2), 0, 2) for r in range(BAND_BLOCKS)]
    soft = []
    for g in range(gh):
        bias = jnp.concatenate([tab_ref[g, entry[r]] for r in range(BAND_BLOCKS)], axis=1)
        s = jnp.where(allowed, qk[g] * (ATT_SCALE * LOG2E) + bias, NEG_INF)
        p = jnp.exp2(s - s.max(axis=1, keepdims=True))
        soft.append((p.sum(axis=1, keepdims=True), p.astype(BF16)))
    pv = [jnp.dot(soft[g][1], v_ref[pl.ds(s0, width), _head(g)], preferred_element_type=F32)
          for g in range(gh)]
    for g in range(gh):
        o_ref[:, _head(g)] = _gated(pv[g] / soft[g][0], g_ref[:, _head(g)]).astype(o_ref.dtype)


def _attention(kind, q, kv, gate, extra, *, b, sq, sk, nh, q0, sk_valid=None, kb=None, gh=8):
    nqt = sq // TQ
    assert nh % gh == 0
    ng = nh // gh
    gw = gh * HEAD_DIM
    q_spec = pl.BlockSpec((TQ, gw), lambda bi, h, i: (bi * nqt + i, h))
    k_spec = pl.BlockSpec((sk, gw), lambda bi, h, i: (bi, h))
    v_spec = pl.BlockSpec((sk, gw), lambda bi, h, i: (bi, ng + h))
    head_vec = pl.BlockSpec((gh, 1, LANE), lambda bi, h, i: (h, 0, 0))
    if kind == "dsa":
        mask, slopes = extra
        nkbt = sk // kb
        body = functools.partial(_dsa_attn_kernel, q0=q0, kb=kb, nkbt=nkbt, gh=gh)
        extra_specs = [pl.BlockSpec((1, 1, nkbt, TQ, kb), lambda bi, h, i: (bi, i, 0, 0, 0)), head_vec]
        extra_args = [mask, slopes]
    elif kind == "stick":
        assert kb % TQ == 0 and q0 % TQ == 0
        body = functools.partial(_stick_kernel, q0=q0, kb=kb, nkbt=sk // kb, gh=gh)
        extra_specs, extra_args = [], []
    else:
        body = functools.partial(_band_kernel, q0=q0, sk_valid=sk_valid, gh=gh)
        extra_specs = [pl.BlockSpec((gh, 3, TQ, TQ), lambda bi, h, i: (h, 0, 0, 0))]
        extra_args = [extra]
    return pl.pallas_call(
        body,
        grid=(b, ng, nqt),
        in_specs=[q_spec, k_spec, v_spec] + extra_specs + [q_spec],
        out_specs=q_spec,
        out_shape=jax.ShapeDtypeStruct((b * sq, nh * HEAD_DIM), BF16),
        compiler_params=_cparams(("parallel", "parallel", "arbitrary")),
        name=kind + "_attention",
    )(q, kv, kv, *extra_args, gate)


def _pad_q(a, b, nq, sq):
    if nq == sq:
        return a
    return jnp.pad(a.reshape(b, nq, -1), ((0, 0), (0, sq - nq), (0, 0))).reshape(b * sq, -1)


def _unpad_q(a, b, nq, sq):
    if nq == sq:
        return a
    return a.reshape(b, sq, -1)[:, :nq].reshape(b * nq, -1)


def _with_cache(cache, new, b, nq, sq, lead):
    width = new.shape[-1]
    parts = [jnp.zeros((b, lead, width), BF16)] if lead else []
    parts += [cache.reshape(b, cache.shape[1], width).astype(BF16), new.reshape(b, nq, width).astype(BF16),
              jnp.zeros((b, sq - nq, width), BF16)]
    return jnp.concatenate(parts, axis=1).reshape(-1, width)


def _kv_rows_kernel(c_ref, n_ref, o_ref, *, lead_blocks, cache_blocks, nq, nsub):
    r = pl.program_id(1)
    rows, width = o_ref.shape[1], o_ref.shape[2]

    @pl.when(r < lead_blocks)
    def _():
        o_ref[...] = jnp.zeros(o_ref.shape, o_ref.dtype)

    @pl.when((r >= lead_blocks) & (r < lead_blocks + cache_blocks))
    def _():
        for c in range(nsub):
            o_ref[0, :, c * LANE:(c + 1) * LANE] = c_ref[pl.ds(c, rows, stride=nsub), :].astype(o_ref.dtype)

    @pl.when(r == lead_blocks + cache_blocks)
    def _():
        o_ref[0, :nq, :] = n_ref[0]
        o_ref[0, nq:, :] = jnp.zeros((rows - nq, width), o_ref.dtype)


def _kv_rows(cache, new, *, b, nq, sq, lead):
    width = new.shape[-1]
    p = cache.shape[1]
    nsub = width // LANE
    assert sq == TQ and p % TQ == 0 and lead % TQ == 0 and nq % 16 == 0
    lead_blocks, cache_blocks = lead // TQ, p // TQ
    cache_block = lambda bi, r: (bi * cache_blocks + jnp.clip(r - lead_blocks, 0, cache_blocks - 1), 0)
    return pl.pallas_call(
        functools.partial(_kv_rows_kernel, lead_blocks=lead_blocks, cache_blocks=cache_blocks, nq=nq, nsub=nsub),
        grid=(b, lead_blocks + cache_blocks + 1),
        in_specs=[pl.BlockSpec((TQ * nsub, LANE), cache_block),
                  pl.BlockSpec((1, nq, width), lambda bi, r: (bi, 0, 0))],
        out_specs=pl.BlockSpec((1, TQ, width), lambda bi, r: (bi, r, 0)),
        out_shape=jax.ShapeDtypeStruct((b, lead + p + sq, width), BF16),
        compiler_params=_cparams(("parallel", "arbitrary")),
        name="kv_rows",
    )(cache.reshape(b * p * nsub, LANE), new.reshape(b, nq, width)).reshape(-1, width)


def _even_layer(x, wts, cache, *, b, nq, q0):
    sq = -(-nq // TQ) * TQ
    na = wts["qa"].shape[1] // HEAD_DIM
    nb = wts["qb"].shape[1] // HEAD_DIM
    h = _rmsnorm(x, wts["norm"], BF16)
    qa = _mm([(h, wts["qa"])], [BF16])
    kva32, kva16 = _mm([(h, wts["kva"])], [F32, BF16])
    ga = _mm([(h, wts["ga"])], [F32])
    idx32, idx16 = _mm([(h, wts["idx"])], [F32, BF16], tm=512, tn=wts["idx"].shape[1])
    qb = _mm([(h, wts["qb"])], [BF16])
    kvb32, kvb16 = _mm([(h, wts["kvb"])], [F32, BF16])
    gb = _mm([(h, wts["gb"])], [F32])

    n_iq = N_IDX_HEADS * IDX_DIM
    ikn = _iknorm(idx32, wts["gk"], n_iq // LANE)
    iw = idx32[:, n_iq + IDX_DIM:n_iq + IDX_DIM + N_IDX_HEADS]
    iwt = _pad_q(iw, b, nq, sq).T

    if cache is None:
        sk = sk_valid = sq
        ka_all, kb_all = kva16, kvb16
        ik_all = ikn.reshape(b, sk, IDX_DIM).astype(BF16)
    else:
        a_kv, a_kidx, b_kv = cache
        sk, sk_valid = q0 + sq, q0 + nq
        ka_all = _kv_rows(a_kv, kva16, b=b, nq=nq, sq=sq, lead=0)
        kb_all = _kv_rows(b_kv, kvb16, b=b, nq=nq, sq=sq, lead=0)
        ik_all = _with_cache(a_kidx, ikn, b, nq, sq, 0).reshape(b, sk, IDX_DIM)
    zeros = jnp.zeros_like(ik_all)
    ik2 = jnp.concatenate([ik_all, zeros, zeros, ik_all], axis=-1).reshape(b * sk, 2 * LANE)

    pad = functools.partial(_pad_q, b=b, nq=nq, sq=sq)
    kb_d = _pick(sk, (512, 384, 256, 128))
    mask = _dsa_mask(pad(idx16), ik2, iwt, b=b, sq=sq, sk=sk, q0=q0, sk_valid=sk_valid,
                     topk=min(TOPK_MAX, sk_valid // 4), kb=kb_d)
    slopes = 2.0 ** (-8.0 * jnp.arange(1, na + 1, dtype=F32) / na)
    slopes = jnp.broadcast_to(slopes[:, None, None], (na, 1, LANE))
    oa = _attention("dsa", pad(qa), ka_all, pad(ga), (mask, slopes), b=b, sq=sq, sk=sk, nh=na, q0=q0, kb=kb_d,
                    gh=4)
    ob = _attention("stick", pad(qb), kb_all, pad(gb), None, b=b, sq=sq, sk=sk, nh=nb, q0=q0,
                    kb=_pick(sk, (256, 384, 128)))
    x_new = _mm([(_unpad_q(oa, b, nq, sq), wts["out_a"]), (_unpad_q(ob, b, nq, sq), wts["out_b"])], [F32], res=x)
    return x_new, kva32, ikn, kvb32


def _odd_layer(x, wts, cache, *, b, nq, q0):
    sq = -(-nq // TQ) * TQ
    nc = wts["q"].shape[1] // HEAD_DIM
    h = _rmsnorm(x, wts["norm"], BF16)
    q = _mm([(h, wts["q"])], [BF16])
    kv32, kv16 = _mm([(h, wts["kv"])], [F32, BF16])
    g = _mm([(h, wts["g"])], [F32])
    if cache is None:
        sk = sk_valid = sq
        kv_all = kv16
    else:
        sk, sk_valid = q0 + sq, q0 + nq
        kv_all = _kv_rows(cache, kv16, b=b, nq=nq, sq=sq, lead=q0 - cache.shape[1])
    o = _attention("band", _pad_q(q, b, nq, sq), kv_all, _pad_q(g, b, nq, sq), wts["tab"],
                   b=b, sq=sq, sk=sk, nh=nc, q0=q0, sk_valid=sk_valid)
    x_new = _mm([(_unpad_q(o, b, nq, sq), wts["out"])], [F32], res=x)
    return x_new, kv32


def _even_weights(norm, w_in, gk, w_out, d):
    wa = d // 2
    n_idx = N_IDX_HEADS * IDX_DIM + IDX_DIM + N_IDX_HEADS
    n_idx_pad = -(-n_idx // LANE) * LANE
    o_idx = 4 * wa
    o_b = o_idx + n_idx
    cut = lambda lo, n: w_in[:, lo:lo + n].astype(BF16)
    return {
        "norm": norm, "gk": gk,
        "qa": cut(0, wa), "kva": cut(wa, 2 * wa), "ga": cut(3 * wa, wa),
        "idx": jnp.pad(cut(o_idx, n_idx), ((0, 0), (0, n_idx_pad - n_idx))),
        "qb": cut(o_b, wa), "kvb": cut(o_b + wa, 2 * wa), "gb": cut(o_b + 3 * wa, wa),
        "out_a": w_out[:wa].astype(BF16), "out_b": w_out[wa:].astype(BF16),
    }


def _odd_weights(norm, w_in, rel_bias, w_out, d):
    nc = rel_bias.shape[0]
    rb = rel_bias * LOG2E
    lo, hi = rb[:, :1], rb[:, 2 * REL_CLIP:]
    by_dist = jnp.concatenate([jnp.broadcast_to(hi, (nc, 2 * TQ - 1 - REL_CLIP)), rb[:, ::-1],
                               jnp.broadcast_to(lo, (nc, TQ - 1 - REL_CLIP))], axis=1)
    window = lambda base: jnp.stack([by_dist[:, 2 * TQ - 1 - base - t:3 * TQ - 1 - base - t]
                                     for t in range(TQ)], axis=1)
    far = jnp.broadcast_to(hi[:, :, None], (nc, TQ, TQ))
    return {
        "norm": norm,
        "q": w_in[:, :d].astype(BF16), "kv": w_in[:, d:3 * d].astype(BF16), "g": w_in[:, 3 * d:].astype(BF16),
        "tab": jnp.stack([far, window(TQ), window(0)], axis=1),
        "out": w_out.astype(BF16),
    }


def kernel(x_prompt, x_sample, cache_a_kv, cache_a_kidx, cache_b_kv, cache_c_kv, norm_e, w_in_e,
           idx_k_gain, w_out_e, norm_o, w_in_o, rel_bias_o, w_out_o, norm_f):
    bp, sp, d = x_prompt.shape
    bs, ts, _ = x_sample.shape
    p_len = cache_a_kv.shape[2]
    depth = norm_e.shape[0] + norm_o.shape[0]
    assert sp % TQ == 0 and p_len % TQ == 0 and ts <= TQ
    assert REL_CLIP <= CHUNK and TQ == 2 * CHUNK

    xp = x_prompt.reshape(bp * sp, d)
    xs = x_sample.reshape(bs * ts, d)
    rows_p = lambda a, *tail: a.reshape(bp, sp, *tail)
    rows_s = lambda a, *tail: a.reshape(bs, ts, *tail)
    outs = {k: [] for k in ("a_kv_p", "a_ix_p", "b_kv_p", "c_kv_p", "a_kv_s", "a_ix_s", "b_kv_s", "c_kv_s")}
    for layer in range(depth):
        j = layer // 2
        if layer % 2 == 0:
            wts = _even_weights(norm_e[j], w_in_e[j], idx_k_gain[j], w_out_e[j], d)
            na = wts["qa"].shape[1] // HEAD_DIM
            nb = wts["qb"].shape[1] // HEAD_DIM
            xp, akv, aix, bkv = _even_layer(xp, wts, None, b=bp, nq=sp, q0=0)
            xs, akv2, aix2, bkv2 = _even_layer(xs, wts, (cache_a_kv[j], cache_a_kidx[j], cache_b_kv[j]),
                                               b=bs, nq=ts, q0=p_len)
            outs["a_kv_p"].append(rows_p(akv, 2, na, HEAD_DIM))
            outs["a_ix_p"].append(rows_p(aix, IDX_DIM))
            outs["b_kv_p"].append(rows_p(bkv, 2, nb, HEAD_DIM))
            outs["a_kv_s"].append(rows_s(akv2, 2, na, HEAD_DIM))
            outs["a_ix_s"].append(rows_s(aix2, IDX_DIM))
            outs["b_kv_s"].append(rows_s(bkv2, 2, nb, HEAD_DIM))
        else:
            wts = _odd_weights(norm_o[j], w_in_o[j], rel_bias_o[j], w_out_o[j], d)
            nc = wts["q"].shape[1] // HEAD_DIM
            wc = min(C_PAST_CHUNKS * CHUNK, sp)
            xp, ckv = _odd_layer(xp, wts, None, b=bp, nq=sp, q0=0)
            xs, ckv2 = _odd_layer(xs, wts, cache_c_kv[j], b=bs, nq=ts, q0=p_len)
            outs["c_kv_p"].append(rows_p(ckv, 2, nc, HEAD_DIM)[:, sp - wc:])
            outs["c_kv_s"].append(rows_s(ckv2, 2, nc, HEAD_DIM))
    y_prompt = _rmsnorm(xp, norm_f, F32).reshape(bp, sp, d)
    y_sample = _rmsnorm(xs, norm_f, F32).reshape(bs, ts, d)
    return (y_prompt, y_sample,
            jnp.stack(outs["a_kv_p"]), jnp.stack(outs["a_ix_p"]), jnp.stack(outs["b_kv_p"]), jnp.stack(outs["c_kv_p"]),
            jnp.stack(outs["a_kv_s"]), jnp.stack(outs["a_ix_s"]), jnp.stack(outs["b_kv_s"]), jnp.stack(outs["c_kv_s"]))
```

```python
import functools

import jax
import jax.numpy as jnp
from jax import lax
from jax.experimental import pallas as pl
from jax.experimental.pallas import tpu as pltpu

F32 = jnp.float32
BF16 = jnp.bfloat16

HEAD_DIM = 128
CHUNK = 64
CHUNK_SHIFT = 6
N_IDX_HEADS = 16
IDX_DIM = 64
TOPK_MAX = 256
C_PAST_CHUNKS = 8
REL_CLIP = 64
EPS = 1e-6
ATT_SCALE = HEAD_DIM ** -0.5
LOG2E = 1.4426950408889634
Q_SCALE = ATT_SCALE * LOG2E
IDX_SCALE = (IDX_DIM ** -0.5) * (N_IDX_HEADS ** -0.5)

TQ = 128
LANE = 128
BAND_BLOCKS = (C_PAST_CHUNKS * CHUNK) // TQ + 1
INT_MIN = -2 ** 31
NEG_INF = float("-inf")
VMEM_LIMIT = 56 * 1024 * 1024


def _cparams(sem):
    return pltpu.CompilerParams(dimension_semantics=sem, vmem_limit_bytes=VMEM_LIMIT)


def _pick(n, candidates):
    for c in candidates:
        if n % c == 0:
            return c
    raise ValueError(f"no block size in {candidates} divides {n}")


def _rmsnorm_kernel(x_ref, g_ref, o_ref):
    x = x_ref[...]
    ms = jnp.mean(x * x, axis=-1, keepdims=True)
    o_ref[...] = (x * lax.rsqrt(ms + EPS) * g_ref[...]).astype(o_ref.dtype)


def _rmsnorm(x, g, out_dtype):
    m, d = x.shape
    tm = _pick(m, (256, 128, 64, 32, 16, 8))
    return pl.pallas_call(
        _rmsnorm_kernel,
        grid=(m // tm,),
        in_specs=[pl.BlockSpec((tm, d), lambda i: (i, 0)),
                  pl.BlockSpec((1, d), lambda i: (0, 0))],
        out_specs=pl.BlockSpec((tm, d), lambda i: (i, 0)),
        out_shape=jax.ShapeDtypeStruct((m, d), out_dtype),
        compiler_params=_cparams(("parallel",)),
        name="rmsnorm",
    )(x, g.reshape(1, d))


def _iknorm_kernel(x_ref, g_ref, o_ref):
    x = x_ref[:, :IDX_DIM]
    ms = jnp.mean(x * x, axis=-1, keepdims=True)
    o_ref[...] = x * lax.rsqrt(ms + EPS) * g_ref[...]


def _iknorm(idx32, gk, col_block):
    m = idx32.shape[0]
    tm = _pick(m, (1024, 512, 256, 128, 64, 32, 16, 8))
    return pl.pallas_call(
        _iknorm_kernel,
        grid=(m // tm,),
        in_specs=[pl.BlockSpec((tm, LANE), lambda i: (i, col_block)),
                  pl.BlockSpec((1, IDX_DIM), lambda i: (0, 0))],
        out_specs=pl.BlockSpec((tm, IDX_DIM), lambda i: (i, 0)),
        out_shape=jax.ShapeDtypeStruct((m, IDX_DIM), F32),
        compiler_params=_cparams(("parallel",)),
        name="iknorm",
    )(idx32, gk.reshape(1, IDX_DIM))


def _mm_kernel(*refs, n_pairs, has_res, n_out, scale):
    acc = None
    for p in range(n_pairs):
        d = jnp.dot(refs[2 * p][...], refs[2 * p + 1][...], preferred_element_type=F32)
        acc = d if acc is None else acc + d
    pos = 2 * n_pairs
    if has_res:
        acc = acc + refs[pos][...]
        pos += 1
    if scale is not None:
        acc = acc * scale
    for o_ref in refs[pos:pos + n_out]:
        o_ref[...] = acc.astype(o_ref.dtype)


def _mm(pairs, out_dtypes, res=None, scale=None, tm=1024, tn=1024):
    m = pairs[0][0].shape[0]
    n = pairs[0][1].shape[1]
    tm = min(tm, m)
    tn = min(tn, n)
    assert m % tm == 0 and n % tn == 0, (m, n, tm, tn)
    in_specs, args = [], []
    for a, w in pairs:
        k = a.shape[1]
        in_specs += [pl.BlockSpec((tm, k), lambda i, j: (i, 0)),
                     pl.BlockSpec((k, tn), lambda i, j: (0, j))]
        args += [a, w]
    if res is not None:
        in_specs.append(pl.BlockSpec((tm, tn), lambda i, j: (i, j)))
        args.append(res)
    outs = pl.pallas_call(
        functools.partial(_mm_kernel, n_pairs=len(pairs), has_res=res is not None,
                          n_out=len(out_dtypes), scale=scale),
        grid=(m // tm, n // tn),
        in_specs=in_specs,
        out_specs=[pl.BlockSpec((tm, tn), lambda i, j: (i, j)) for _ in out_dtypes],
        out_shape=[jax.ShapeDtypeStruct((m, n), dt) for dt in out_dtypes],
        compiler_params=_cparams(("parallel", "arbitrary")),
        name="matmul",
    )(*args)
    return outs[0] if len(out_dtypes) == 1 else tuple(outs)


def _nt_dot(a, b):
    return lax.dot_general(a, b, (((1,), (1,)), ((), ())), preferred_element_type=F32)


def _dsa_mask_kernel(iq_ref, ik_ref, iwt_ref, o_ref, keys_ref, *, q0, kb, nkbt, sk_valid, topk):
    i = pl.program_id(1)
    nkb = jnp.minimum(nkbt, (q0 + (i + 1) * TQ + kb - 1) // kb)
    w = iwt_ref[...] * IDX_SCALE
    qchunk = (q0 + i * TQ + lax.broadcasted_iota(jnp.int32, (1, TQ), 1)) >> CHUNK_SHIFT

    def build(t, _):
        off = pl.multiple_of(t * kb, kb)
        acc = jnp.zeros((kb, TQ), F32)
        for h in range(N_IDX_HEADS):
            e, p = h % 2, h // 2
            ikh = ik_ref[pl.ds(off, kb), e * LANE:(e + 1) * LANE]
            s = _nt_dot(ikh, iq_ref[:, p * LANE:(p + 1) * LANE])
            acc = acc + jnp.maximum(s, 0.0) * w[h:h + 1, :]
        acc = acc + 0.0
        bits = lax.bitcast_convert_type(acc, jnp.int32)
        key = jnp.where(bits < 0, bits ^ 0x7FFFFFFF, bits)
        kpos = off + lax.broadcasted_iota(jnp.int32, (kb, TQ), 0)
        adm = ((kpos >> CHUNK_SHIFT) <= qchunk) & (kpos < sk_valid)
        keys_ref[pl.ds(off, kb), :] = jnp.where(adm, key, INT_MIN)
        return 0

    lax.fori_loop(0, nkb, build, 0)

    def count(pred):
        def body(t, acc8):
            off = pl.multiple_of(t * kb, kb)
            hit = pred(keys_ref[pl.ds(off, kb), :]).astype(jnp.int32)
            return acc8 + hit.reshape(kb // 8, 8, TQ).sum(axis=0)
        acc8 = lax.fori_loop(0, nkb, body, jnp.zeros((8, TQ), jnp.int32))
        return acc8.sum(axis=0, keepdims=True)

    thr = jnp.where(count(lambda kx: kx >= 0) >= topk, 0, INT_MIN).astype(jnp.int32)

    def bit_step(t, thr):
        cand = thr | (jnp.int32(1) << (30 - t))
        return jnp.where(count(lambda kx: kx >= cand) >= topk, cand, thr)

    thr = lax.fori_loop(0, 31, bit_step, thr)
    need = (topk - count(lambda kx: kx > thr)).astype(F32)

    tri = (lax.broadcasted_iota(jnp.int32, (kb, kb), 0)
           >= lax.broadcasted_iota(jnp.int32, (kb, kb), 1)).astype(BF16)

    def emit(t, seen):
        off = pl.multiple_of(t * kb, kb)
        kx = keys_ref[pl.ds(off, kb), :]
        eq = kx == thr
        rank = jnp.dot(tri, eq.astype(BF16), preferred_element_type=F32) + seen
        sel = ((kx > thr) | (eq & (rank <= need))) & (kx != INT_MIN)
        o_ref[0, 0, t] = jnp.where(sel, 0.0, NEG_INF).T.astype(o_ref.dtype)
        return rank[kb - 1:kb, :]

    lax.fori_loop(0, nkb, emit, jnp.zeros((1, TQ), F32))

    def fill(t, _):
        o_ref[0, 0, t] = jnp.full((TQ, kb), NEG_INF, o_ref.dtype)
        return 0

    lax.fori_loop(nkb, nkbt, fill, 0)


def _dsa_mask(idx16, ik2, iwt, *, b, sq, sk, q0, sk_valid, topk, kb):
    nqt, nkbt = sq // TQ, sk // kb
    n_iq = N_IDX_HEADS * IDX_DIM
    return pl.pallas_call(
        functools.partial(_dsa_mask_kernel, q0=q0, kb=kb, nkbt=nkbt, sk_valid=sk_valid, topk=topk),
        grid=(b, nqt),
        in_specs=[pl.BlockSpec((TQ, n_iq), lambda bi, i: (bi * nqt + i, 0)),
                  pl.BlockSpec((sk, 2 * LANE), lambda bi, i: (bi, 0)),
                  pl.BlockSpec((N_IDX_HEADS, TQ), lambda bi, i: (0, bi * nqt + i))],
        out_specs=pl.BlockSpec((1, 1, nkbt, TQ, kb), lambda bi, i: (bi, i, 0, 0, 0)),
        out_shape=jax.ShapeDtypeStruct((b, nqt, nkbt, TQ, kb), BF16),
        scratch_shapes=[pltpu.VMEM((sk, TQ), jnp.int32)],
        compiler_params=_cparams(("parallel", "arbitrary")),
        name="dsa_mask",
    )(idx16, ik2, iwt)


def _gated(o, g):
    return o * (g * jax.nn.sigmoid(g))


def _head(g):
    return slice(g * HEAD_DIM, (g + 1) * HEAD_DIM)


def _dsa_attn_kernel(q_ref, k_ref, v_ref, m_ref, sl_ref, g_ref, o_ref, *, q0, kb, nkbt, gh):
    i = pl.program_id(2)
    tq = q_ref.shape[0]
    nkb = jnp.minimum(nkbt, (q0 + (i + 1) * tq + kb - 1) // kb)
    qpos = q0 + i * tq + lax.broadcasted_iota(jnp.int32, (tq, kb), 0)
    col = lax.broadcasted_iota(jnp.int32, (tq, kb), 1)

    def body(t, carry):
        off = pl.multiple_of(t * kb, kb)
        dist = jnp.abs(qpos - (off + col)).astype(F32)
        mask = m_ref[0, 0, t, :tq, :].astype(F32)
        qk = [_nt_dot(q_ref[:, _head(g)], k_ref[pl.ds(off, kb), _head(g)]) for g in range(gh)]
        soft = []
        for g in range(gh):
            m, l, _ = carry[g]
            s = qk[g] - (sl_ref[g][:, :1] * LOG2E) * dist + mask
            m_new = jnp.maximum(m, s.max(axis=1, keepdims=True))
            m_use = jnp.where(m_new == NEG_INF, 0.0, m_new)
            alpha = jnp.exp2(m - m_use)
            p = jnp.exp2(s - m_use)
            soft.append((m_new, alpha * l + p.sum(axis=1, keepdims=True), alpha, p.astype(BF16)))
        pv = [jnp.dot(soft[g][3], v_ref[pl.ds(off, kb), _head(g)], preferred_element_type=F32)
              for g in range(gh)]
        return tuple((soft[g][0], soft[g][1], soft[g][2] * carry[g][2] + pv[g]) for g in range(gh))

    init = tuple((jnp.full((tq, 1), NEG_INF, F32), jnp.zeros((tq, 1), F32),
                  jnp.zeros((tq, HEAD_DIM), F32)) for _ in range(gh))
    res = lax.fori_loop(0, nkb, body, init)
    for g in range(gh):
        _, l, acc = res[g]
        o_ref[:, _head(g)] = _gated(acc / l, g_ref[:, _head(g)]).astype(o_ref.dtype)


def _stick_kernel(q_ref, k_ref, v_ref, g_ref, o_ref, *, q0, kb, nkbt, gh):
    i = pl.program_id(2)
    tq = q_ref.shape[0]
    nkb = jnp.minimum(nkbt, (q0 + (i + 1) * tq + kb - 1) // kb)
    later = (lax.broadcasted_iota(jnp.int32, (kb, kb), 0)
             > lax.broadcasted_iota(jnp.int32, (kb, kb), 1)).astype(BF16)
    qpos = q0 + i * tq + lax.broadcasted_iota(jnp.int32, (tq, kb), 0)
    col = lax.broadcasted_iota(jnp.int32, (tq, kb), 1)

    def block(off, carry, diagonal):
        causal = (off + col) < qpos
        qk = [_nt_dot(q_ref[:, _head(g)], k_ref[pl.ds(off, kb), _head(g)]) for g in range(gh)]
        logs = []
        for g in range(gh):
            z2 = qk[g]
            sp = jnp.log2(1.0 + jnp.exp2(-jnp.abs(z2)))
            drop = jnp.maximum(z2, 0.0) + sp
            if diagonal:
                drop = jnp.where(causal, drop, 0.0)
            logs.append((jnp.minimum(z2, 0.0) - sp - carry[g][0], drop))
        suffix = []
        for g in range(gh):
            drop = logs[g][1]
            hi = drop.astype(BF16)
            lo = (drop - hi.astype(F32)).astype(BF16)
            suffix.append(jnp.dot(hi, later, preferred_element_type=F32)
                          + jnp.dot(lo, later, preferred_element_type=F32))
        w = []
        for g in range(gh):
            wg = jnp.exp2(logs[g][0] - suffix[g])
            if diagonal:
                wg = jnp.where(causal, wg, 0.0)
            w.append(wg.astype(BF16))
        pv = [jnp.dot(w[g], v_ref[pl.ds(off, kb), _head(g)], preferred_element_type=F32) for g in range(gh)]
        return tuple((carry[g][0] + suffix[g][:, :1] + logs[g][1][:, :1], carry[g][1] + pv[g])
                     for g in range(gh))

    init = tuple((jnp.zeros((tq, 1), F32), jnp.zeros((tq, HEAD_DIM), F32)) for _ in range(gh))
    first = block(pl.multiple_of((nkb - 1) * kb, kb), init, True)
    res = lax.fori_loop(1, nkb, lambda t, c: block(pl.multiple_of((nkb - 1 - t) * kb, kb), c, False), first)
    for g in range(gh):
        o_ref[:, _head(g)] = _gated(res[g][1], g_ref[:, _head(g)]).astype(o_ref.dtype)


def _band_kernel(q_ref, k_ref, v_ref, tab_ref, g_ref, o_ref, *, q0, sk_valid, gh):
    i = pl.program_id(2)
    tq = q_ref.shape[0]
    jabs = (q0 + i * tq) // TQ
    back = BAND_BLOCKS - 1
    shift = back - jnp.minimum(jabs, back)
    s0 = pl.multiple_of(jnp.maximum(jabs - back, 0) * TQ, TQ)
    width = BAND_BLOCKS * TQ
    qpos = q0 + i * tq + lax.broadcasted_iota(jnp.int32, (tq, width), 0)
    kpos = s0 + lax.broadcasted_iota(jnp.int32, (tq, width), 1)
    qc, kc = qpos >> CHUNK_SHIFT, kpos >> CHUNK_SHIFT
    allowed = (kc <= qc) & (kc >= qc - C_PAST_CHUNKS) & (kpos < sk_valid)
    qk = [_nt_dot(q_ref[:, _head(g)], k_ref[pl.ds(s0, width), _head(g)]) for g in range(gh)]
    entry = [jnp.clip(r + shift - (back - 2), 0, 2) for r in range(BAND_BLOCKS)]
    soft = []
    for g in range(gh):
        bias = jnp.concatenate([tab_ref[g, entry[r], :tq, :] for r in range(BAND_BLOCKS)], axis=1)
        s = jnp.where(allowed, qk[g] + bias, NEG_INF)
        p = jnp.exp2(s - s.max(axis=1, keepdims=True))
        soft.append((p.sum(axis=1, keepdims=True), p.astype(BF16)))
    pv = [jnp.dot(soft[g][1], v_ref[pl.ds(s0, width), _head(g)], preferred_element_type=F32)
          for g in range(gh)]
    for g in range(gh):
        o_ref[:, _head(g)] = _gated(pv[g] / soft[g][0], g_ref[:, _head(g)]).astype(o_ref.dtype)


def _attention(kind, q, kv, gate, extra, *, b, sq, sk, nh, q0, sk_valid=None, kb=None, gh=8):
    tq = min(TQ, sq)
    nqt = sq // tq
    assert nh % gh == 0 and sq % tq == 0 and q0 % TQ == 0 and tq % 16 == 0
    ng = nh // gh
    gw = gh * HEAD_DIM
    q_spec = pl.BlockSpec((tq, gw), lambda bi, h, i: (bi * nqt + i, h))
    k_spec = pl.BlockSpec((sk, gw), lambda bi, h, i: (bi, h))
    v_spec = pl.BlockSpec((sk, gw), lambda bi, h, i: (bi, ng + h))
    head_vec = pl.BlockSpec((gh, 1, LANE), lambda bi, h, i: (h, 0, 0))
    if kind == "dsa":
        mask, slopes = extra
        nkbt = sk // kb
        body = functools.partial(_dsa_attn_kernel, q0=q0, kb=kb, nkbt=nkbt, gh=gh)
        extra_specs = [pl.BlockSpec((1, 1, nkbt, TQ, kb), lambda bi, h, i: (bi, i, 0, 0, 0)), head_vec]
        extra_args = [mask, slopes]
    elif kind == "stick":
        assert kb % tq == 0 and q0 % tq == 0
        body = functools.partial(_stick_kernel, q0=q0, kb=kb, nkbt=sk // kb, gh=gh)
        extra_specs, extra_args = [], []
    else:
        body = functools.partial(_band_kernel, q0=q0, sk_valid=sk_valid, gh=gh)
        extra_specs = [pl.BlockSpec((gh, 3, TQ, TQ), lambda bi, h, i: (h, 0, 0, 0))]
        extra_args = [extra]
    return pl.pallas_call(
        body,
        grid=(b, ng, nqt),
        in_specs=[q_spec, k_spec, v_spec] + extra_specs + [q_spec],
        out_specs=q_spec,
        out_shape=jax.ShapeDtypeStruct((b * sq, nh * HEAD_DIM), BF16),
        compiler_params=_cparams(("parallel", "parallel", "arbitrary")),
        name=kind + "_attention",
    )(q, kv, kv, *extra_args, gate)


def _pad_q(a, b, nq, sq):
    if nq == sq:
        return a
    return jnp.pad(a.reshape(b, nq, -1), ((0, 0), (0, sq - nq), (0, 0))).reshape(b * sq, -1)


def _with_cache(cache, new, b, nq, sq, lead):
    width = new.shape[-1]
    parts = [jnp.zeros((b, lead, width), BF16)] if lead else []
    parts += [cache.reshape(b, cache.shape[1], width).astype(BF16), new.reshape(b, nq, width).astype(BF16),
              jnp.zeros((b, sq - nq, width), BF16)]
    return jnp.concatenate(parts, axis=1).reshape(-1, width)


def _kv_rows_kernel(c_ref, n_ref, o_ref, *, lead_blocks, cache_blocks, nq, nsub):
    r = pl.program_id(1)
    rows, width = o_ref.shape[1], o_ref.shape[2]

    @pl.when(r < lead_blocks)
    def _():
        o_ref[...] = jnp.zeros(o_ref.shape, o_ref.dtype)

    @pl.when((r >= lead_blocks) & (r < lead_blocks + cache_blocks))
    def _():
        by_sub = jnp.swapaxes(c_ref[...], 0, 1)
        for c in range(nsub):
            o_ref[0, :, c * LANE:(c + 1) * LANE] = by_sub[c].astype(o_ref.dtype)

    @pl.when(r == lead_blocks + cache_blocks)
    def _():
        o_ref[0, :nq, :] = n_ref[0]
        o_ref[0, nq:, :] = jnp.zeros((rows - nq, width), o_ref.dtype)


def _kv_rows(cache, new, *, b, nq, sq, lead):
    width = new.shape[-1]
    p = cache.shape[1]
    nsub = width // LANE
    assert sq == TQ and p % TQ == 0 and lead % TQ == 0 and nq % 16 == 0
    lead_blocks, cache_blocks = lead // TQ, p // TQ
    cache_block = lambda bi, r: (bi * cache_blocks + jnp.clip(r - lead_blocks, 0, cache_blocks - 1), 0, 0)
    return pl.pallas_call(
        functools.partial(_kv_rows_kernel, lead_blocks=lead_blocks, cache_blocks=cache_blocks, nq=nq, nsub=nsub),
        grid=(b, lead_blocks + cache_blocks + 1),
        in_specs=[pl.BlockSpec((TQ, nsub, LANE), cache_block),
                  pl.BlockSpec((1, nq, width), lambda bi, r: (bi, 0, 0))],
        out_specs=pl.BlockSpec((1, TQ, width), lambda bi, r: (bi, r, 0)),
        out_shape=jax.ShapeDtypeStruct((b, lead + p + sq, width), BF16),
        compiler_params=_cparams(("parallel", "arbitrary")),
        name="kv_rows",
    )(cache.reshape(b * p, nsub, LANE), new.reshape(b, nq, width)).reshape(-1, width)


def _even_layer(x, wts, cache, *, b, nq, q0):
    sq = -(-nq // TQ) * TQ
    na = wts["qa"].shape[1] // HEAD_DIM
    nb = wts["qb"].shape[1] // HEAD_DIM
    h = _rmsnorm(x, wts["norm"], BF16)
    qa = _mm([(h, wts["qa"])], [BF16], scale=Q_SCALE)
    kva32, kva16 = _mm([(h, wts["kva"])], [F32, BF16])
    ga = _mm([(h, wts["ga"])], [F32])
    idx32, idx16 = _mm([(h, wts["idx"])], [F32, BF16], tm=512, tn=wts["idx"].shape[1])
    qb = _mm([(h, wts["qb"])], [BF16], scale=Q_SCALE)
    kvb32, kvb16 = _mm([(h, wts["kvb"])], [F32, BF16])
    gb = _mm([(h, wts["gb"])], [F32])

    n_iq = N_IDX_HEADS * IDX_DIM
    ikn = _iknorm(idx32, wts["gk"], n_iq // LANE)
    iw = idx32[:, n_iq + IDX_DIM:n_iq + IDX_DIM + N_IDX_HEADS]
    iwt = _pad_q(iw, b, nq, sq).T

    if cache is None:
        sk = sk_valid = sq
        ka_all, kb_all = kva16, kvb16
        ik_all = ikn.reshape(b, sk, IDX_DIM).astype(BF16)
    else:
        a_kv, a_kidx, b_kv = cache
        sk, sk_valid = q0 + sq, q0 + nq
        ka_all = _kv_rows(a_kv, kva16, b=b, nq=nq, sq=sq, lead=0)
        kb_all = _kv_rows(b_kv, kvb16, b=b, nq=nq, sq=sq, lead=0)
        ik_all = _with_cache(a_kidx, ikn, b, nq, sq, 0).reshape(b, sk, IDX_DIM)
    zeros = jnp.zeros_like(ik_all)
    ik2 = jnp.concatenate([ik_all, zeros, zeros, ik_all], axis=-1).reshape(b * sk, 2 * LANE)

    pad = functools.partial(_pad_q, b=b, nq=nq, sq=sq)
    kb_d = _pick(sk, (512, 384, 256, 128))
    mask = _dsa_mask(pad(idx16), ik2, iwt, b=b, sq=sq, sk=sk, q0=q0, sk_valid=sk_valid,
                     topk=min(TOPK_MAX, sk_valid // 4), kb=kb_d)
    slopes = 2.0 ** (-8.0 * jnp.arange(1, na + 1, dtype=F32) / na)
    slopes = jnp.broadcast_to(slopes[:, None, None], (na, 1, LANE))
    oa = _attention("dsa", qa, ka_all, ga, (mask, slopes), b=b, sq=nq, sk=sk, nh=na, q0=q0, kb=kb_d, gh=4)
    ob = _attention("stick", qb, kb_all, gb, None, b=b, sq=nq, sk=sk, nh=nb, q0=q0,
                    kb=_pick(sk, (256, 384, 128)))
    x_new = _mm([(oa, wts["out_a"]), (ob, wts["out_b"])], [F32], res=x)
    return x_new, kva32, ikn, kvb32


def _odd_layer(x, wts, cache, *, b, nq, q0):
    sq = -(-nq // TQ) * TQ
    nc = wts["q"].shape[1] // HEAD_DIM
    h = _rmsnorm(x, wts["norm"], BF16)
    q = _mm([(h, wts["q"])], [BF16], scale=Q_SCALE)
    kv32, kv16 = _mm([(h, wts["kv"])], [F32, BF16])
    g = _mm([(h, wts["g"])], [F32])
    if cache is None:
        sk = sk_valid = sq
        kv_all = kv16
    else:
        sk, sk_valid = q0 + sq, q0 + nq
        kv_all = _kv_rows(cache, kv16, b=b, nq=nq, sq=sq, lead=q0 - cache.shape[1])
    o = _attention("band", q, kv_all, g, wts["tab"], b=b, sq=nq, sk=sk, nh=nc, q0=q0, sk_valid=sk_valid)
    x_new = _mm([(o, wts["out"])], [F32], res=x)
    return x_new, kv32


def _even_weights(norm, w_in, gk, w_out, d):
    wa = d // 2
    n_idx = N_IDX_HEADS * IDX_DIM + IDX_DIM + N_IDX_HEADS
    n_idx_pad = -(-n_idx // LANE) * LANE
    o_idx = 4 * wa
    o_b = o_idx + n_idx
    cut = lambda lo, n: w_in[:, lo:lo + n].astype(BF16)
    return {
        "norm": norm, "gk": gk,
        "qa": cut(0, wa), "kva": cut(wa, 2 * wa), "ga": cut(3 * wa, wa),
        "idx": jnp.pad(cut(o_idx, n_idx), ((0, 0), (0, n_idx_pad - n_idx))),
        "qb": cut(o_b, wa), "kvb": cut(o_b + wa, 2 * wa), "gb": cut(o_b + 3 * wa, wa),
        "out_a": w_out[:wa].astype(BF16), "out_b": w_out[wa:].astype(BF16),
    }


def _odd_weights(norm, w_in, rel_bias, w_out, d):
    nc = rel_bias.shape[0]
    rb = rel_bias * LOG2E
    lo, hi = rb[:, :1], rb[:, 2 * REL_CLIP:]
    by_dist = jnp.concatenate([jnp.broadcast_to(hi, (nc, 2 * TQ - 1 - REL_CLIP)), rb[:, ::-1],
                               jnp.broadcast_to(lo, (nc, TQ - 1 - REL_CLIP))], axis=1)
    window = lambda base: jnp.stack([by_dist[:, 2 * TQ - 1 - base - t:3 * TQ - 1 - base - t]
                                     for t in range(TQ)], axis=1)
    far = jnp.broadcast_to(hi[:, :, None], (nc, TQ, TQ))
    return {
        "norm": norm,
        "q": w_in[:, :d].astype(BF16), "kv": w_in[:, d:3 * d].astype(BF16), "g": w_in[:, 3 * d:].astype(BF16),
        "tab": jnp.stack([far, window(TQ), window(0)], axis=1),
        "out": w_out.astype(BF16),
    }


def kernel(x_prompt, x_sample, cache_a_kv, cache_a_kidx, cache_b_kv, cache_c_kv, norm_e, w_in_e,
           idx_k_gain, w_out_e, norm_o, w_in_o, rel_bias_o, w_out_o, norm_f):
    bp, sp, d = x_prompt.shape
    bs, ts, _ = x_sample.shape
    p_len = cache_a_kv.shape[2]
    depth = norm_e.shape[0] + norm_o.shape[0]
    assert sp % TQ == 0 and p_len % TQ == 0 and ts <= TQ
    assert REL_CLIP <= CHUNK and TQ == 2 * CHUNK

    xp = x_prompt.reshape(bp * sp, d)
    xs = x_sample.reshape(bs * ts, d)
    rows_p = lambda a, *tail: a.reshape(bp, sp, *tail)
    rows_s = lambda a, *tail: a.reshape(bs, ts, *tail)
    outs = {k: [] for k in ("a_kv_p", "a_ix_p", "b_kv_p", "c_kv_p", "a_kv_s", "a_ix_s", "b_kv_s", "c_kv_s")}
    for layer in range(depth):
        j = layer // 2
        if layer % 2 == 0:
            wts = _even_weights(norm_e[j], w_in_e[j], idx_k_gain[j], w_out_e[j], d)
            na = wts["qa"].shape[1] // HEAD_DIM
            nb = wts["qb"].shape[1] // HEAD_DIM
            xp, akv, aix, bkv = _even_layer(xp, wts, None, b=bp, nq=sp, q0=0)
            xs, akv2, aix2, bkv2 = _even_layer(xs, wts, (cache_a_kv[j], cache_a_kidx[j], cache_b_kv[j]),
                                               b=bs, nq=ts, q0=p_len)
            outs["a_kv_p"].append(rows_p(akv, 2, na, HEAD_DIM))
            outs["a_ix_p"].append(rows_p(aix, IDX_DIM))
            outs["b_kv_p"].append(rows_p(bkv, 2, nb, HEAD_DIM))
            outs["a_kv_s"].append(rows_s(akv2, 2, na, HEAD_DIM))
            outs["a_ix_s"].append(rows_s(aix2, IDX_DIM))
            outs["b_kv_s"].append(rows_s(bkv2, 2, nb, HEAD_DIM))
        else:
            wts = _odd_weights(norm_o[j], w_in_o[j], rel_bias_o[j], w_out_o[j], d)
            nc = wts["q"].shape[1] // HEAD_DIM
            wc = min(C_PAST_CHUNKS * CHUNK, sp)
            xp, ckv = _odd_layer(xp, wts, None, b=bp, nq=sp, q0=0)
            xs, ckv2 = _odd_layer(xs, wts, cache_c_kv[j], b=bs, nq=ts, q0=p_len)
            outs["c_kv_p"].append(rows_p(ckv, 2, nc, HEAD_DIM)[:, sp - wc:])
            outs["c_kv_s"].append(rows_s(ckv2, 2, nc, HEAD_DIM))
    y_prompt = _rmsnorm(xp, norm_f, F32).reshape(bp, sp, d)
    y_sample = _rmsnorm(xs, norm_f, F32).reshape(bs, ts, d)
    return (y_prompt, y_sample,
            jnp.stack(outs["a_kv_p"]), jnp.stack(outs["a_ix_p"]), jnp.stack(outs["b_kv_p"]), jnp.stack(outs["c_kv_p"]),
            jnp.stack(outs["a_kv_s"]), jnp.stack(outs["a_ix_s"]), jnp.stack(outs["b_kv_s"]), jnp.stack(outs["c_kv_s"]))
```

```python
import functools

import jax
import jax.numpy as jnp
from jax import lax
from jax.experimental import pallas as pl
from jax.experimental.pallas import tpu as pltpu

F32 = jnp.float32
BF16 = jnp.bfloat16

HEAD_DIM = 128
CHUNK = 64
CHUNK_SHIFT = 6
N_IDX_HEADS = 16
IDX_DIM = 64
TOPK_MAX = 256
C_PAST_CHUNKS = 8
REL_CLIP = 64
EPS = 1e-6
ATT_SCALE = HEAD_DIM ** -0.5
LOG2E = 1.4426950408889634
Q_SCALE = ATT_SCALE * LOG2E
IDX_SCALE = (IDX_DIM ** -0.5) * (N_IDX_HEADS ** -0.5)

TQ = 128
LANE = 128
BAND_BLOCKS = (C_PAST_CHUNKS * CHUNK) // TQ + 1
INT_MIN = -2 ** 31
NEG_INF = float("-inf")
VMEM_LIMIT = 56 * 1024 * 1024


def _cparams(sem):
    return pltpu.CompilerParams(dimension_semantics=sem, vmem_limit_bytes=VMEM_LIMIT)


def _pick(n, candidates):
    for c in candidates:
        if n % c == 0:
            return c
    raise ValueError(f"no block size in {candidates} divides {n}")


def _rmsnorm_kernel(x_ref, g_ref, o_ref):
    x = x_ref[...]
    ms = jnp.mean(x * x, axis=-1, keepdims=True)
    o_ref[...] = (x * lax.rsqrt(ms + EPS) * g_ref[...]).astype(o_ref.dtype)


def _rmsnorm(x, g, out_dtype):
    m, d = x.shape
    tm = _pick(m, (256, 128, 64, 32, 16, 8))
    return pl.pallas_call(
        _rmsnorm_kernel,
        grid=(m // tm,),
        in_specs=[pl.BlockSpec((tm, d), lambda i: (i, 0)),
                  pl.BlockSpec((1, d), lambda i: (0, 0))],
        out_specs=pl.BlockSpec((tm, d), lambda i: (i, 0)),
        out_shape=jax.ShapeDtypeStruct((m, d), out_dtype),
        compiler_params=_cparams(("parallel",)),
        name="rmsnorm",
    )(x, g.reshape(1, d))


def _iknorm_kernel(x_ref, g_ref, o_ref):
    x = x_ref[:, :IDX_DIM]
    ms = jnp.mean(x * x, axis=-1, keepdims=True)
    o_ref[...] = x * lax.rsqrt(ms + EPS) * g_ref[...]


def _iknorm(idx32, gk, col_block):
    m = idx32.shape[0]
    tm = _pick(m, (1024, 512, 256, 128, 64, 32, 16, 8))
    return pl.pallas_call(
        _iknorm_kernel,
        grid=(m // tm,),
        in_specs=[pl.BlockSpec((tm, LANE), lambda i: (i, col_block)),
                  pl.BlockSpec((1, IDX_DIM), lambda i: (0, 0))],
        out_specs=pl.BlockSpec((tm, IDX_DIM), lambda i: (i, 0)),
        out_shape=jax.ShapeDtypeStruct((m, IDX_DIM), F32),
        compiler_params=_cparams(("parallel",)),
        name="iknorm",
    )(idx32, gk.reshape(1, IDX_DIM))


def _mm_kernel(*refs, n_pairs, has_res, n_out, scale):
    acc = None
    for p in range(n_pairs):
        d = jnp.dot(refs[2 * p][...], refs[2 * p + 1][...], preferred_element_type=F32)
        acc = d if acc is None else acc + d
    pos = 2 * n_pairs
    if has_res:
        acc = acc + refs[pos][...]
        pos += 1
    if scale is not None:
        acc = acc * scale
    for o_ref in refs[pos:pos + n_out]:
        o_ref[...] = acc.astype(o_ref.dtype)


def _mm(pairs, out_dtypes, res=None, scale=None, tm=1024, tn=1024):
    m = pairs[0][0].shape[0]
    n = pairs[0][1].shape[1]
    tm = min(tm, m)
    tn = min(tn, n)
    assert m % tm == 0 and n % tn == 0, (m, n, tm, tn)
    in_specs, args = [], []
    for a, w in pairs:
        k = a.shape[1]
        in_specs += [pl.BlockSpec((tm, k), lambda i, j: (i, 0)),
                     pl.BlockSpec((k, tn), lambda i, j: (0, j))]
        args += [a, w]
    if res is not None:
        in_specs.append(pl.BlockSpec((tm, tn), lambda i, j: (i, j)))
        args.append(res)
    outs = pl.pallas_call(
        functools.partial(_mm_kernel, n_pairs=len(pairs), has_res=res is not None,
                          n_out=len(out_dtypes), scale=scale),
        grid=(m // tm, n // tn),
        in_specs=in_specs,
        out_specs=[pl.BlockSpec((tm, tn), lambda i, j: (i, j)) for _ in out_dtypes],
        out_shape=[jax.ShapeDtypeStruct((m, n), dt) for dt in out_dtypes],
        compiler_params=_cparams(("parallel", "arbitrary")),
        name="matmul",
    )(*args)
    return outs[0] if len(out_dtypes) == 1 else tuple(outs)


def _nt_dot(a, b):
    return lax.dot_general(a, b, (((1,), (1,)), ((), ())), preferred_element_type=F32)


def _dsa_mask_kernel(iq_ref, ik_ref, iwt_ref, o_ref, keys_ref, *, q0, kb, nkbt, sk_valid, topk):
    i = pl.program_id(1)
    nkb = jnp.minimum(nkbt, (q0 + (i + 1) * TQ + kb - 1) // kb)
    w = iwt_ref[...] * IDX_SCALE
    qchunk = (q0 + i * TQ + lax.broadcasted_iota(jnp.int32, (1, TQ), 1)) >> CHUNK_SHIFT

    def build(t, _):
        off = pl.multiple_of(t * kb, kb)
        acc = jnp.zeros((kb, TQ), F32)
        for h in range(N_IDX_HEADS):
            e, p = h % 2, h // 2
            ikh = ik_ref[pl.ds(off, kb), e * LANE:(e + 1) * LANE]
            s = _nt_dot(ikh, iq_ref[:, p * LANE:(p + 1) * LANE])
            acc = acc + jnp.maximum(s, 0.0) * w[h:h + 1, :]
        acc = acc + 0.0
        bits = lax.bitcast_convert_type(acc, jnp.int32)
        key = jnp.where(bits < 0, bits ^ 0x7FFFFFFF, bits)
        kpos = off + lax.broadcasted_iota(jnp.int32, (kb, TQ), 0)
        adm = ((kpos >> CHUNK_SHIFT) <= qchunk) & (kpos < sk_valid)
        keys_ref[pl.ds(off, kb), :] = jnp.where(adm, key, INT_MIN)
        return 0

    lax.fori_loop(0, nkb, build, 0)

    def count(pred):
        def body(t, acc8):
            off = pl.multiple_of(t * kb, kb)
            hit = pred(keys_ref[pl.ds(off, kb), :]).astype(jnp.int32)
            return acc8 + hit.reshape(kb // 8, 8, TQ).sum(axis=0)
        acc8 = lax.fori_loop(0, nkb, body, jnp.zeros((8, TQ), jnp.int32))
        return acc8.sum(axis=0, keepdims=True)

    thr = jnp.where(count(lambda kx: kx >= 0) >= topk, 0, INT_MIN).astype(jnp.int32)

    def bit_step(t, thr):
        cand = thr | (jnp.int32(1) << (30 - t))
        return jnp.where(count(lambda kx: kx >= cand) >= topk, cand, thr)

    thr = lax.fori_loop(0, 31, bit_step, thr)
    n_ge = count(lambda kx: kx >= thr)
    tie = jnp.max(jnp.where(thr == INT_MIN, 0, n_ge)) > topk

    @pl.when(jnp.logical_not(tie))
    def _():
        def emit(t, _):
            off = pl.multiple_of(t * kb, kb)
            kx = keys_ref[pl.ds(off, kb), :]
            sel = (kx >= thr) & (kx != INT_MIN)
            o_ref[0, 0, t] = jnp.where(sel, 0.0, NEG_INF).T.astype(o_ref.dtype)
            return 0

        lax.fori_loop(0, nkb, emit, 0)

    @pl.when(tie)
    def _():
        need = (topk - count(lambda kx: kx > thr)).astype(F32)
        tri = (lax.broadcasted_iota(jnp.int32, (kb, kb), 0)
               >= lax.broadcasted_iota(jnp.int32, (kb, kb), 1)).astype(BF16)

        def emit(t, seen):
            off = pl.multiple_of(t * kb, kb)
            kx = keys_ref[pl.ds(off, kb), :]
            eq = kx == thr
            rank = jnp.dot(tri, eq.astype(BF16), preferred_element_type=F32) + seen
            sel = ((kx > thr) | (eq & (rank <= need))) & (kx != INT_MIN)
            o_ref[0, 0, t] = jnp.where(sel, 0.0, NEG_INF).T.astype(o_ref.dtype)
            return rank[kb - 1:kb, :]

        lax.fori_loop(0, nkb, emit, jnp.zeros((1, TQ), F32))

    def fill(t, _):
        o_ref[0, 0, t] = jnp.full((TQ, kb), NEG_INF, o_ref.dtype)
        return 0

    lax.fori_loop(nkb, nkbt, fill, 0)


def _dsa_mask(idx16, ik2, iwt, *, b, sq, sk, q0, sk_valid, topk, kb):
    nqt, nkbt = sq // TQ, sk // kb
    n_iq = N_IDX_HEADS * IDX_DIM
    return pl.pallas_call(
        functools.partial(_dsa_mask_kernel, q0=q0, kb=kb, nkbt=nkbt, sk_valid=sk_valid, topk=topk),
        grid=(b, nqt),
        in_specs=[pl.BlockSpec((TQ, n_iq), lambda bi, i: (bi * nqt + i, 0)),
                  pl.BlockSpec((sk, 2 * LANE), lambda bi, i: (bi, 0)),
                  pl.BlockSpec((N_IDX_HEADS, TQ), lambda bi, i: (0, bi * nqt + i))],
        out_specs=pl.BlockSpec((1, 1, nkbt, TQ, kb), lambda bi, i: (bi, i, 0, 0, 0)),
        out_shape=jax.ShapeDtypeStruct((b, nqt, nkbt, TQ, kb), BF16),
        scratch_shapes=[pltpu.VMEM((sk, TQ), jnp.int32)],
        compiler_params=_cparams(("parallel", "arbitrary")),
        name="dsa_mask",
    )(idx16, ik2, iwt)


def _gated(o, g):
    return o * (g * jax.nn.sigmoid(g))


def _head(g):
    return slice(g * HEAD_DIM, (g + 1) * HEAD_DIM)


def _dsa_attn_kernel(q_ref, k_ref, v_ref, m_ref, sl_ref, g_ref, o_ref, *, q0, kb, nkbt, gh):
    i = pl.program_id(2)
    tq = q_ref.shape[0]
    nkb = jnp.minimum(nkbt, (q0 + (i + 1) * tq + kb - 1) // kb)
    qpos = q0 + i * tq + lax.broadcasted_iota(jnp.int32, (tq, kb), 0)
    col = lax.broadcasted_iota(jnp.int32, (tq, kb), 1)

    def body(t, carry):
        off = pl.multiple_of(t * kb, kb)
        dist = jnp.abs(qpos - (off + col)).astype(F32)
        if tq > TQ:
            mask = jnp.concatenate([m_ref[0, u, t] for u in range(tq // TQ)], axis=0).astype(F32)
        else:
            mask = m_ref[0, 0, t, :tq, :].astype(F32)
        qk = [_nt_dot(q_ref[:, _head(g)], k_ref[pl.ds(off, kb), _head(g)]) for g in range(gh)]
        soft = []
        for g in range(gh):
            m, l, _ = carry[g]
            s = qk[g] - (sl_ref[g][:, :1] * LOG2E) * dist + mask
            m_new = jnp.maximum(m, s.max(axis=1, keepdims=True))
            m_use = jnp.where(m_new == NEG_INF, 0.0, m_new)
            alpha = jnp.exp2(m - m_use)
            p = jnp.exp2(s - m_use)
            soft.append((m_new, alpha * l + p.sum(axis=1, keepdims=True), alpha, p.astype(BF16)))
        pv = [jnp.dot(soft[g][3], v_ref[pl.ds(off, kb), _head(g)], preferred_element_type=F32)
              for g in range(gh)]
        return tuple((soft[g][0], soft[g][1], soft[g][2] * carry[g][2] + pv[g]) for g in range(gh))

    init = tuple((jnp.full((tq, 1), NEG_INF, F32), jnp.zeros((tq, 1), F32),
                  jnp.zeros((tq, HEAD_DIM), F32)) for _ in range(gh))
    res = lax.fori_loop(0, nkb, body, init)
    for g in range(gh):
        _, l, acc = res[g]
        o_ref[:, _head(g)] = _gated(acc / l, g_ref[:, _head(g)]).astype(o_ref.dtype)


def _stick_kernel(q_ref, k_ref, v_ref, g_ref, o_ref, *, q0, kb, nkbt, gh):
    i = pl.program_id(2)
    tq = q_ref.shape[0]
    nkb = jnp.minimum(nkbt, (q0 + (i + 1) * tq + kb - 1) // kb)
    from_here = (lax.broadcasted_iota(jnp.int32, (kb, kb), 0)
                 >= lax.broadcasted_iota(jnp.int32, (kb, kb), 1)).astype(BF16)
    qpos = q0 + i * tq + lax.broadcasted_iota(jnp.int32, (tq, kb), 0)
    col = lax.broadcasted_iota(jnp.int32, (tq, kb), 1)

    def block(off, carry, diagonal):
        causal = (off + col) < qpos
        qk = [_nt_dot(q_ref[:, _head(g)], k_ref[pl.ds(off, kb), _head(g)]) for g in range(gh)]
        drops = []
        for g in range(gh):
            z2 = qk[g]
            neg_abs = lax.bitcast_convert_type(lax.bitcast_convert_type(z2, jnp.int32) | INT_MIN, F32)
            drop = jnp.maximum(z2, 0.0) + jnp.log2(1.0 + jnp.exp2(neg_abs))
            if diagonal:
                drop = jnp.where(causal, drop, 0.0)
            drops.append(drop)
        incl = []
        for g in range(gh):
            hi = drops[g].astype(BF16)
            lo = (drops[g] - hi.astype(F32)).astype(BF16)
            incl.append(jnp.dot(hi, from_here, preferred_element_type=F32)
                        + jnp.dot(lo, from_here, preferred_element_type=F32))
        w = []
        for g in range(gh):
            wg = jnp.exp2(qk[g] - carry[g][0] - incl[g])
            if diagonal:
                wg = jnp.where(causal, wg, 0.0)
            w.append(wg.astype(BF16))
        pv = [jnp.dot(w[g], v_ref[pl.ds(off, kb), _head(g)], preferred_element_type=F32) for g in range(gh)]
        return tuple((carry[g][0] + incl[g][:, :1], carry[g][1] + pv[g]) for g in range(gh))

    init = tuple((jnp.zeros((tq, 1), F32), jnp.zeros((tq, HEAD_DIM), F32)) for _ in range(gh))
    first = block(pl.multiple_of((nkb - 1) * kb, kb), init, True)
    res = lax.fori_loop(1, nkb, lambda t, c: block(pl.multiple_of((nkb - 1 - t) * kb, kb), c, False), first)
    for g in range(gh):
        o_ref[:, _head(g)] = _gated(res[g][1], g_ref[:, _head(g)]).astype(o_ref.dtype)


def _band_kernel(q_ref, k_ref, v_ref, tab_ref, g_ref, o_ref, *, q0, sk_valid, gh):
    i = pl.program_id(2)
    tq = q_ref.shape[0]
    jabs = (q0 + i * tq) // TQ
    back = BAND_BLOCKS - 1
    shift = back - jnp.minimum(jabs, back)
    s0 = pl.multiple_of(jnp.maximum(jabs - back, 0) * TQ, TQ)
    width = BAND_BLOCKS * TQ
    qpos = q0 + i * tq + lax.broadcasted_iota(jnp.int32, (tq, width), 0)
    kpos = s0 + lax.broadcasted_iota(jnp.int32, (tq, width), 1)
    qc, kc = qpos >> CHUNK_SHIFT, kpos >> CHUNK_SHIFT
    allowed = (kc <= qc) & (kc >= qc - C_PAST_CHUNKS) & (kpos < sk_valid)
    qk = [_nt_dot(q_ref[:, _head(g)], k_ref[pl.ds(s0, width), _head(g)]) for g in range(gh)]
    entry = [jnp.clip(r + shift - (back - 2), 0, 2) for r in range(BAND_BLOCKS)]
    soft = []
    for g in range(gh):
        bias = jnp.concatenate([tab_ref[g, entry[r], :tq, :] for r in range(BAND_BLOCKS)], axis=1)
        s = jnp.where(allowed, qk[g] + bias, NEG_INF)
        p = jnp.exp2(s - s.max(axis=1, keepdims=True))
        soft.append((p.sum(axis=1, keepdims=True), p.astype(BF16)))
    pv = [jnp.dot(soft[g][1], v_ref[pl.ds(s0, width), _head(g)], preferred_element_type=F32)
          for g in range(gh)]
    for g in range(gh):
        o_ref[:, _head(g)] = _gated(pv[g] / soft[g][0], g_ref[:, _head(g)]).astype(o_ref.dtype)


def _attention(kind, q, kv, gate, extra, *, b, sq, sk, nh, q0, sk_valid=None, kb=None, gh=8):
    tq = min(TQ if kind == "band" else 2 * TQ, sq)
    nqt = sq // tq
    assert nh % gh == 0 and sq % tq == 0 and q0 % TQ == 0 and tq % 16 == 0
    ng = nh // gh
    gw = gh * HEAD_DIM
    q_spec = pl.BlockSpec((tq, gw), lambda bi, h, i: (bi * nqt + i, h))
    k_spec = pl.BlockSpec((sk, gw), lambda bi, h, i: (bi, h))
    v_spec = pl.BlockSpec((sk, gw), lambda bi, h, i: (bi, ng + h))
    head_vec = pl.BlockSpec((gh, 1, LANE), lambda bi, h, i: (h, 0, 0))
    if kind == "dsa":
        mask, slopes = extra
        nkbt = sk // kb
        body = functools.partial(_dsa_attn_kernel, q0=q0, kb=kb, nkbt=nkbt, gh=gh)
        extra_specs = [pl.BlockSpec((1, max(tq // TQ, 1), nkbt, TQ, kb), lambda bi, h, i: (bi, i, 0, 0, 0)),
                       head_vec]
        extra_args = [mask, slopes]
    elif kind == "stick":
        assert kb % tq == 0 and q0 % tq == 0
        body = functools.partial(_stick_kernel, q0=q0, kb=kb, nkbt=sk // kb, gh=gh)
        extra_specs, extra_args = [], []
    else:
        body = functools.partial(_band_kernel, q0=q0, sk_valid=sk_valid, gh=gh)
        extra_specs = [pl.BlockSpec((gh, 3, TQ, TQ), lambda bi, h, i: (h, 0, 0, 0))]
        extra_args = [extra]
    return pl.pallas_call(
        body,
        grid=(b, ng, nqt),
        in_specs=[q_spec, k_spec, v_spec] + extra_specs + [q_spec],
        out_specs=q_spec,
        out_shape=jax.ShapeDtypeStruct((b * sq, nh * HEAD_DIM), BF16),
        compiler_params=_cparams(("parallel", "parallel", "arbitrary")),
        name=kind + "_attention",
    )(q, kv, kv, *extra_args, gate)


def _pad_q(a, b, nq, sq):
    if nq == sq:
        return a
    return jnp.pad(a.reshape(b, nq, -1), ((0, 0), (0, sq - nq), (0, 0))).reshape(b * sq, -1)


def _with_cache(cache, new, b, nq, sq, lead):
    width = new.shape[-1]
    parts = [jnp.zeros((b, lead, width), BF16)] if lead else []
    parts += [cache.reshape(b, cache.shape[1], width).astype(BF16), new.reshape(b, nq, width).astype(BF16),
              jnp.zeros((b, sq - nq, width), BF16)]
    return jnp.concatenate(parts, axis=1).reshape(-1, width)


def _kv_rows_kernel(c_ref, n_ref, o_ref, *, lead_blocks, cache_blocks, nq, nsub):
    r = pl.program_id(1)
    rows, width = o_ref.shape[1], o_ref.shape[2]

    @pl.when(r < lead_blocks)
    def _():
        o_ref[...] = jnp.zeros(o_ref.shape, o_ref.dtype)

    @pl.when((r >= lead_blocks) & (r < lead_blocks + cache_blocks))
    def _():
        by_sub = jnp.swapaxes(c_ref[...], 0, 1)
        for c in range(nsub):
            o_ref[0, :, c * LANE:(c + 1) * LANE] = by_sub[c].astype(o_ref.dtype)

    @pl.when(r == lead_blocks + cache_blocks)
    def _():
        o_ref[0, :nq, :] = n_ref[0]
        o_ref[0, nq:, :] = jnp.zeros((rows - nq, width), o_ref.dtype)


def _kv_rows(cache, new, *, b, nq, sq, lead):
    width = new.shape[-1]
    p = cache.shape[1]
    nsub = width // LANE
    assert sq == TQ and p % TQ == 0 and lead % TQ == 0 and nq % 16 == 0
    lead_blocks, cache_blocks = lead // TQ, p // TQ
    cache_block = lambda bi, r: (bi * cache_blocks + jnp.clip(r - lead_blocks, 0, cache_blocks - 1), 0, 0)
    return pl.pallas_call(
        functools.partial(_kv_rows_kernel, lead_blocks=lead_blocks, cache_blocks=cache_blocks, nq=nq, nsub=nsub),
        grid=(b, lead_blocks + cache_blocks + 1),
        in_specs=[pl.BlockSpec((TQ, nsub, LANE), cache_block),
                  pl.BlockSpec((1, nq, width), lambda bi, r: (bi, 0, 0))],
        out_specs=pl.BlockSpec((1, TQ, width), lambda bi, r: (bi, r, 0)),
        out_shape=jax.ShapeDtypeStruct((b, lead + p + sq, width), BF16),
        compiler_params=_cparams(("parallel", "arbitrary")),
        name="kv_rows",
    )(cache.reshape(b * p, nsub, LANE), new.reshape(b, nq, width)).reshape(-1, width)


def _even_layer(x, wts, cache, *, b, nq, q0):
    sq = -(-nq // TQ) * TQ
    na = wts["qa"].shape[1] // HEAD_DIM
    nb = wts["qb"].shape[1] // HEAD_DIM
    h = _rmsnorm(x, wts["norm"], BF16)
    qa = _mm([(h, wts["qa"])], [BF16], scale=Q_SCALE)
    kva32, kva16 = _mm([(h, wts["kva"])], [F32, BF16])
    ga = _mm([(h, wts["ga"])], [F32])
    idx32, idx16 = _mm([(h, wts["idx"])], [F32, BF16], tm=512, tn=wts["idx"].shape[1])
    qb = _mm([(h, wts["qb"])], [BF16], scale=Q_SCALE)
    kvb32, kvb16 = _mm([(h, wts["kvb"])], [F32, BF16])
    gb = _mm([(h, wts["gb"])], [F32])

    n_iq = N_IDX_HEADS * IDX_DIM
    ikn = _iknorm(idx32, wts["gk"], n_iq // LANE)
    iw = idx32[:, n_iq + IDX_DIM:n_iq + IDX_DIM + N_IDX_HEADS]
    iwt = _pad_q(iw, b, nq, sq).T

    if cache is None:
        sk = sk_valid = sq
        ka_all, kb_all = kva16, kvb16
        ik_all = ikn.reshape(b, sk, IDX_DIM).astype(BF16)
    else:
        a_kv, a_kidx, b_kv = cache
        sk, sk_valid = q0 + sq, q0 + nq
        ka_all = _kv_rows(a_kv, kva16, b=b, nq=nq, sq=sq, lead=0)
        kb_all = _kv_rows(b_kv, kvb16, b=b, nq=nq, sq=sq, lead=0)
        ik_all = _with_cache(a_kidx, ikn, b, nq, sq, 0).reshape(b, sk, IDX_DIM)
    zeros = jnp.zeros_like(ik_all)
    ik2 = jnp.concatenate([ik_all, zeros, zeros, ik_all], axis=-1).reshape(b * sk, 2 * LANE)

    pad = functools.partial(_pad_q, b=b, nq=nq, sq=sq)
    kb_d = _pick(sk, (512, 384, 256, 128))
    mask = _dsa_mask(pad(idx16), ik2, iwt, b=b, sq=sq, sk=sk, q0=q0, sk_valid=sk_valid,
                     topk=min(TOPK_MAX, sk_valid // 4), kb=kb_d)
    slopes = 2.0 ** (-8.0 * jnp.arange(1, na + 1, dtype=F32) / na)
    slopes = jnp.broadcast_to(slopes[:, None, None], (na, 1, LANE))
    oa = _attention("dsa", qa, ka_all, ga, (mask, slopes), b=b, sq=nq, sk=sk, nh=na, q0=q0, kb=kb_d, gh=4)
    ob = _attention("stick", qb, kb_all, gb, None, b=b, sq=nq, sk=sk, nh=nb, q0=q0,
                    kb=_pick(sk, (256, 384, 128)))
    x_new = _mm([(oa, wts["out_a"]), (ob, wts["out_b"])], [F32], res=x)
    return x_new, kva32, ikn, kvb32


def _odd_layer(x, wts, cache, *, b, nq, q0):
    sq = -(-nq // TQ) * TQ
    nc = wts["q"].shape[1] // HEAD_DIM
    h = _rmsnorm(x, wts["norm"], BF16)
    q = _mm([(h, wts["q"])], [BF16], scale=Q_SCALE)
    kv32, kv16 = _mm([(h, wts["kv"])], [F32, BF16])
    g = _mm([(h, wts["g"])], [F32])
    if cache is None:
        sk = sk_valid = sq
        kv_all = kv16
    else:
        sk, sk_valid = q0 + sq, q0 + nq
        kv_all = _kv_rows(cache, kv16, b=b, nq=nq, sq=sq, lead=q0 - cache.shape[1])
    o = _attention("band", q, kv_all, g, wts["tab"], b=b, sq=nq, sk=sk, nh=nc, q0=q0, sk_valid=sk_valid)
    x_new = _mm([(o, wts["out"])], [F32], res=x)
    return x_new, kv32


def _even_weights(norm, w_in, gk, w_out, d):
    wa = d // 2
    n_idx = N_IDX_HEADS * IDX_DIM + IDX_DIM + N_IDX_HEADS
    n_idx_pad = -(-n_idx // LANE) * LANE
    o_idx = 4 * wa
    o_b = o_idx + n_idx
    cut = lambda lo, n: w_in[:, lo:lo + n].astype(BF16)
    return {
        "norm": norm, "gk": gk,
        "qa": cut(0, wa), "kva": cut(wa, 2 * wa), "ga": cut(3 * wa, wa),
        "idx": jnp.pad(cut(o_idx, n_idx), ((0, 0), (0, n_idx_pad - n_idx))),
        "qb": cut(o_b, wa), "kvb": cut(o_b + wa, 2 * wa), "gb": cut(o_b + 3 * wa, wa),
        "out_a": w_out[:wa].astype(BF16), "out_b": w_out[wa:].astype(BF16),
    }


def _odd_weights(norm, w_in, rel_bias, w_out, d):
    nc = rel_bias.shape[0]
    rb = rel_bias * LOG2E
    lo, hi = rb[:, :1], rb[:, 2 * REL_CLIP:]
    by_dist = jnp.concatenate([jnp.broadcast_to(hi, (nc, 2 * TQ - 1 - REL_CLIP)), rb[:, ::-1],
                               jnp.broadcast_to(lo, (nc, TQ - 1 - REL_CLIP))], axis=1)
    window = lambda base: jnp.stack([by_dist[:, 2 * TQ - 1 - base - t:3 * TQ - 1 - base - t]
                                     for t in range(TQ)], axis=1)
    far = jnp.broadcast_to(hi[:, :, None], (nc, TQ, TQ))
    return {
        "norm": norm,
        "q": w_in[:, :d].astype(BF16), "kv": w_in[:, d:3 * d].astype(BF16), "g": w_in[:, 3 * d:].astype(BF16),
        "tab": jnp.stack([far, window(TQ), window(0)], axis=1),
        "out": w_out.astype(BF16),
    }


def kernel(x_prompt, x_sample, cache_a_kv, cache_a_kidx, cache_b_kv, cache_c_kv, norm_e, w_in_e,
           idx_k_gain, w_out_e, norm_o, w_in_o, rel_bias_o, w_out_o, norm_f):
    bp, sp, d = x_prompt.shape
    bs, ts, _ = x_sample.shape
    p_len = cache_a_kv.shape[2]
    depth = norm_e.shape[0] + norm_o.shape[0]
    assert sp % TQ == 0 and p_len % TQ == 0 and ts <= TQ
    assert REL_CLIP <= CHUNK and TQ == 2 * CHUNK

    xp = x_prompt.reshape(bp * sp, d)
    xs = x_sample.reshape(bs * ts, d)
    rows_p = lambda a, *tail: a.reshape(bp, sp, *tail)
    rows_s = lambda a, *tail: a.reshape(bs, ts, *tail)
    outs = {k: [] for k in ("a_kv_p", "a_ix_p", "b_kv_p", "c_kv_p", "a_kv_s", "a_ix_s", "b_kv_s", "c_kv_s")}
    for layer in range(depth):
        j = layer // 2
        if layer % 2 == 0:
            wts = _even_weights(norm_e[j], w_in_e[j], idx_k_gain[j], w_out_e[j], d)
            na = wts["qa"].shape[1] // HEAD_DIM
            nb = wts["qb"].shape[1] // HEAD_DIM
            xp, akv, aix, bkv = _even_layer(xp, wts, None, b=bp, nq=sp, q0=0)
            xs, akv2, aix2, bkv2 = _even_layer(xs, wts, (cache_a_kv[j], cache_a_kidx[j], cache_b_kv[j]),
                                               b=bs, nq=ts, q0=p_len)
            outs["a_kv_p"].append(rows_p(akv, 2, na, HEAD_DIM))
            outs["a_ix_p"].append(rows_p(aix, IDX_DIM))
            outs["b_kv_p"].append(rows_p(bkv, 2, nb, HEAD_DIM))
            outs["a_kv_s"].append(rows_s(akv2, 2, na, HEAD_DIM))
            outs["a_ix_s"].append(rows_s(aix2, IDX_DIM))
            outs["b_kv_s"].append(rows_s(bkv2, 2, nb, HEAD_DIM))
        else:
            wts = _odd_weights(norm_o[j], w_in_o[j], rel_bias_o[j], w_out_o[j], d)
            nc = wts["q"].shape[1] // HEAD_DIM
            wc = min(C_PAST_CHUNKS * CHUNK, sp)
            xp, ckv = _odd_layer(xp, wts, None, b=bp, nq=sp, q0=0)
            xs, ckv2 = _odd_layer(xs, wts, cache_c_kv[j], b=bs, nq=ts, q0=p_len)
            outs["c_kv_p"].append(rows_p(ckv, 2, nc, HEAD_DIM)[:, sp - wc:])
            outs["c_kv_s"].append(rows_s(ckv2, 2, nc, HEAD_DIM))
    y_prompt = _rmsnorm(xp, norm_f, F32).reshape(bp, sp, d)
    y_sample = _rmsnorm(xs, norm_f, F32).reshape(bs, ts, d)
    return (y_prompt, y_sample,
            jnp.stack(outs["a_kv_p"]), jnp.stack(outs["a_ix_p"]), jnp.stack(outs["b_kv_p"]), jnp.stack(outs["c_kv_p"]),
            jnp.stack(outs["a_kv_s"]), jnp.stack(outs["a_ix_s"]), jnp.stack(outs["b_kv_s"]), jnp.stack(outs["c_kv_s"]))
```

```python
import functools

import jax
import jax.numpy as jnp
from jax import lax
from jax.experimental import pallas as pl
from jax.experimental.pallas import tpu as pltpu

F32 = jnp.float32
BF16 = jnp.bfloat16

HEAD_DIM = 128
CHUNK = 64
CHUNK_SHIFT = 6
N_IDX_HEADS = 16
IDX_DIM = 64
TOPK_MAX = 256
C_PAST_CHUNKS = 8
REL_CLIP = 64
EPS = 1e-6
ATT_SCALE = HEAD_DIM ** -0.5
LOG2E = 1.4426950408889634
Q_SCALE = ATT_SCALE * LOG2E
IDX_SCALE = (IDX_DIM ** -0.5) * (N_IDX_HEADS ** -0.5)

TQ = 128
LANE = 128
BAND_BLOCKS = (C_PAST_CHUNKS * CHUNK) // TQ + 1
INT_MIN = -2 ** 31
NEG_INF = float("-inf")
KV_ROWS = 256
VMEM_LIMIT = 56 * 1024 * 1024


def _cparams(sem):
    return pltpu.CompilerParams(dimension_semantics=sem, vmem_limit_bytes=VMEM_LIMIT)


def _pick(n, candidates):
    for c in candidates:
        if n % c == 0:
            return c
    raise ValueError(f"no block size in {candidates} divides {n}")


def _rmsnorm_kernel(x_ref, g_ref, o_ref):
    x = x_ref[...]
    ms = jnp.mean(x * x, axis=-1, keepdims=True)
    o_ref[...] = (x * lax.rsqrt(ms + EPS) * g_ref[...]).astype(o_ref.dtype)


def _rmsnorm(x, g, out_dtype):
    m, d = x.shape
    tm = _pick(m, (256, 128, 64, 32, 16, 8))
    return pl.pallas_call(
        _rmsnorm_kernel,
        grid=(m // tm,),
        in_specs=[pl.BlockSpec((tm, d), lambda i: (i, 0)),
                  pl.BlockSpec((1, d), lambda i: (0, 0))],
        out_specs=pl.BlockSpec((tm, d), lambda i: (i, 0)),
        out_shape=jax.ShapeDtypeStruct((m, d), out_dtype),
        compiler_params=_cparams(("parallel",)),
        name="rmsnorm",
    )(x, g.reshape(1, d))


def _iknorm_kernel(x_ref, g_ref, o_ref):
    x = x_ref[:, :IDX_DIM]
    ms = jnp.mean(x * x, axis=-1, keepdims=True)
    o_ref[...] = x * lax.rsqrt(ms + EPS) * g_ref[...]


def _iknorm(idx32, gk, col_block):
    m = idx32.shape[0]
    tm = _pick(m, (1024, 512, 256, 128, 64, 32, 16, 8))
    return pl.pallas_call(
        _iknorm_kernel,
        grid=(m // tm,),
        in_specs=[pl.BlockSpec((tm, LANE), lambda i: (i, col_block)),
                  pl.BlockSpec((1, IDX_DIM), lambda i: (0, 0))],
        out_specs=pl.BlockSpec((tm, IDX_DIM), lambda i: (i, 0)),
        out_shape=jax.ShapeDtypeStruct((m, IDX_DIM), F32),
        compiler_params=_cparams(("parallel",)),
        name="iknorm",
    )(idx32, gk.reshape(1, IDX_DIM))


def _mm_kernel(*refs, n_pairs, has_res, n_out, scale):
    acc = None
    for p in range(n_pairs):
        d = jnp.dot(refs[2 * p][...], refs[2 * p + 1][...], preferred_element_type=F32)
        acc = d if acc is None else acc + d
    pos = 2 * n_pairs
    if has_res:
        acc = acc + refs[pos][...]
        pos += 1
    if scale is not None:
        acc = acc * scale
    for o_ref in refs[pos:pos + n_out]:
        o_ref[...] = acc.astype(o_ref.dtype)


def _mm(pairs, out_dtypes, res=None, scale=None, tm=1024, tn=1024):
    m = pairs[0][0].shape[0]
    n = pairs[0][1].shape[1]
    tm = min(tm, m)
    tn = min(tn, n)
    assert m % tm == 0 and n % tn == 0, (m, n, tm, tn)
    in_specs, args = [], []
    for a, w in pairs:
        k = a.shape[1]
        in_specs += [pl.BlockSpec((tm, k), lambda i, j: (i, 0)),
                     pl.BlockSpec((k, tn), lambda i, j: (0, j))]
        args += [a, w]
    if res is not None:
        in_specs.append(pl.BlockSpec((tm, tn), lambda i, j: (i, j)))
        args.append(res)
    outs = pl.pallas_call(
        functools.partial(_mm_kernel, n_pairs=len(pairs), has_res=res is not None,
                          n_out=len(out_dtypes), scale=scale),
        grid=(m // tm, n // tn),
        in_specs=in_specs,
        out_specs=[pl.BlockSpec((tm, tn), lambda i, j: (i, j)) for _ in out_dtypes],
        out_shape=[jax.ShapeDtypeStruct((m, n), dt) for dt in out_dtypes],
        compiler_params=_cparams(("parallel", "arbitrary")),
        name="matmul",
    )(*args)
    return outs[0] if len(out_dtypes) == 1 else tuple(outs)


def _nt_dot(a, b):
    return lax.dot_general(a, b, (((1,), (1,)), ((), ())), preferred_element_type=F32)


def _dsa_mask_kernel(iq_ref, ik_ref, iwt_ref, o_ref, keys_ref, *, q0, kb, nkbt, sk_valid, topk):
    i = pl.program_id(1)
    nkb = jnp.minimum(nkbt, (q0 + (i + 1) * TQ + kb - 1) // kb)
    w = iwt_ref[...] * IDX_SCALE
    qchunk = (q0 + i * TQ + lax.broadcasted_iota(jnp.int32, (1, TQ), 1)) >> CHUNK_SHIFT

    def build(t, _):
        off = pl.multiple_of(t * kb, kb)
        acc = jnp.zeros((kb, TQ), F32)
        for h in range(N_IDX_HEADS):
            e, p = h % 2, h // 2
            ikh = ik_ref[pl.ds(off, kb), e * LANE:(e + 1) * LANE]
            s = _nt_dot(ikh, iq_ref[:, p * LANE:(p + 1) * LANE])
            acc = acc + jnp.maximum(s, 0.0) * w[h:h + 1, :]
        acc = acc + 0.0
        bits = lax.bitcast_convert_type(acc, jnp.int32)
        key = jnp.where(bits < 0, bits ^ 0x7FFFFFFF, bits)
        kpos = off + lax.broadcasted_iota(jnp.int32, (kb, TQ), 0)
        adm = ((kpos >> CHUNK_SHIFT) <= qchunk) & (kpos < sk_valid)
        keys_ref[pl.ds(off, kb), :] = jnp.where(adm, key, INT_MIN)
        return 0

    lax.fori_loop(0, nkb, build, 0)

    def count(pred):
        def body(t, acc8):
            off = pl.multiple_of(t * kb, kb)
            hit = pred(keys_ref[pl.ds(off, kb), :]).astype(jnp.int32)
            return acc8 + hit.reshape(kb // 8, 8, TQ).sum(axis=0)
        acc8 = lax.fori_loop(0, nkb, body, jnp.zeros((8, TQ), jnp.int32))
        return acc8.sum(axis=0, keepdims=True)

    thr = jnp.where(count(lambda kx: kx >= 0) >= topk, 0, INT_MIN).astype(jnp.int32)

    def bit_step(t, thr):
        cand = thr | (jnp.int32(1) << (30 - t))
        return jnp.where(count(lambda kx: kx >= cand) >= topk, cand, thr)

    thr = lax.fori_loop(0, 31, bit_step, thr)
    n_ge = count(lambda kx: kx >= thr)
    tie = jnp.max(jnp.where(thr == INT_MIN, 0, n_ge)) > topk

    @pl.when(jnp.logical_not(tie))
    def _():
        def emit(t, _):
            off = pl.multiple_of(t * kb, kb)
            kx = keys_ref[pl.ds(off, kb), :]
            sel = (kx >= thr) & (kx != INT_MIN)
            o_ref[0, 0, t] = jnp.where(sel, 0.0, NEG_INF).T.astype(o_ref.dtype)
            return 0

        lax.fori_loop(0, nkb, emit, 0)

    @pl.when(tie)
    def _():
        need = (topk - count(lambda kx: kx > thr)).astype(F32)
        tri = (lax.broadcasted_iota(jnp.int32, (kb, kb), 0)
               >= lax.broadcasted_iota(jnp.int32, (kb, kb), 1)).astype(BF16)

        def emit(t, seen):
            off = pl.multiple_of(t * kb, kb)
            kx = keys_ref[pl.ds(off, kb), :]
            eq = kx == thr
            rank = jnp.dot(tri, eq.astype(BF16), preferred_element_type=F32) + seen
            sel = ((kx > thr) | (eq & (rank <= need))) & (kx != INT_MIN)
            o_ref[0, 0, t] = jnp.where(sel, 0.0, NEG_INF).T.astype(o_ref.dtype)
            return rank[kb - 1:kb, :]

        lax.fori_loop(0, nkb, emit, jnp.zeros((1, TQ), F32))

    def fill(t, _):
        o_ref[0, 0, t] = jnp.full((TQ, kb), NEG_INF, o_ref.dtype)
        return 0

    lax.fori_loop(nkb, nkbt, fill, 0)


def _dsa_mask(idx16, ik2, iwt, *, b, sq, sk, q0, sk_valid, topk, kb):
    nqt, nkbt = sq // TQ, sk // kb
    n_iq = N_IDX_HEADS * IDX_DIM
    return pl.pallas_call(
        functools.partial(_dsa_mask_kernel, q0=q0, kb=kb, nkbt=nkbt, sk_valid=sk_valid, topk=topk),
        grid=(b, nqt),
        in_specs=[pl.BlockSpec((TQ, n_iq), lambda bi, i: (bi * nqt + i, 0)),
                  pl.BlockSpec((sk, 2 * LANE), lambda bi, i: (bi, 0)),
                  pl.BlockSpec((N_IDX_HEADS, TQ), lambda bi, i: (0, bi * nqt + i))],
        out_specs=pl.BlockSpec((1, 1, nkbt, TQ, kb), lambda bi, i: (bi, i, 0, 0, 0)),
        out_shape=jax.ShapeDtypeStruct((b, nqt, nkbt, TQ, kb), BF16),
        scratch_shapes=[pltpu.VMEM((sk, TQ), jnp.int32)],
        compiler_params=_cparams(("parallel", "arbitrary")),
        name="dsa_mask",
    )(idx16, ik2, iwt)


def _gated(o, g):
    return o * (g * jax.nn.sigmoid(g))


def _head(g):
    return slice(g * HEAD_DIM, (g + 1) * HEAD_DIM)


def _dsa_attn_kernel(q_ref, k_ref, v_ref, m_ref, sl_ref, g_ref, o_ref, *, q0, kb, nkbt, gh):
    i = pl.program_id(2)
    tq = q_ref.shape[0]
    nkb = jnp.minimum(nkbt, (q0 + (i + 1) * tq + kb - 1) // kb)
    qpos = q0 + i * tq + lax.broadcasted_iota(jnp.int32, (tq, kb), 0)
    col = lax.broadcasted_iota(jnp.int32, (tq, kb), 1)

    def body(t, carry):
        off = pl.multiple_of(t * kb, kb)
        dist = jnp.abs(qpos - (off + col)).astype(F32)
        if tq > TQ:
            mask = jnp.concatenate([m_ref[0, u, t] for u in range(tq // TQ)], axis=0).astype(F32)
        else:
            mask = m_ref[0, 0, t, :tq, :].astype(F32)
        qk = [_nt_dot(q_ref[:, _head(g)], k_ref[pl.ds(off, kb), _head(g)]) for g in range(gh)]
        soft = []
        for g in range(gh):
            m, l, _ = carry[g]
            s = qk[g] - (sl_ref[g][:, :1] * LOG2E) * dist + mask
            m_new = jnp.maximum(m, s.max(axis=1, keepdims=True))
            m_use = jnp.where(m_new == NEG_INF, 0.0, m_new)
            alpha = jnp.exp2(m - m_use)
            p = jnp.exp2(s - m_use)
            soft.append((m_new, alpha * l + p.sum(axis=1, keepdims=True), alpha, p.astype(BF16)))
        pv = [jnp.dot(soft[g][3], v_ref[pl.ds(off, kb), _head(g)], preferred_element_type=F32)
              for g in range(gh)]
        return tuple((soft[g][0], soft[g][1], soft[g][2] * carry[g][2] + pv[g]) for g in range(gh))

    init = tuple((jnp.full((tq, 1), NEG_INF, F32), jnp.zeros((tq, 1), F32),
                  jnp.zeros((tq, HEAD_DIM), F32)) for _ in range(gh))
    res = lax.fori_loop(0, nkb, body, init)
    for g in range(gh):
        _, l, acc = res[g]
        o_ref[:, _head(g)] = _gated(acc / l, g_ref[:, _head(g)]).astype(o_ref.dtype)


def _stick_kernel(q_ref, k_ref, v_ref, g_ref, o_ref, *, q0, kb, nkbt, gh):
    i = pl.program_id(2)
    tq = q_ref.shape[0]
    nkb = jnp.minimum(nkbt, (q0 + (i + 1) * tq + kb - 1) // kb)
    from_here = (lax.broadcasted_iota(jnp.int32, (kb, kb), 0)
                 >= lax.broadcasted_iota(jnp.int32, (kb, kb), 1)).astype(BF16)
    qpos = q0 + i * tq + lax.broadcasted_iota(jnp.int32, (tq, kb), 0)
    col = lax.broadcasted_iota(jnp.int32, (tq, kb), 1)

    def block(off, carry, diagonal):
        causal = (off + col) < qpos
        qk = [_nt_dot(q_ref[:, _head(g)], k_ref[pl.ds(off, kb), _head(g)]) for g in range(gh)]
        drops = []
        for g in range(gh):
            z2 = qk[g]
            neg_abs = lax.bitcast_convert_type(lax.bitcast_convert_type(z2, jnp.int32) | INT_MIN, F32)
            drop = jnp.maximum(z2, 0.0) + jnp.log2(1.0 + jnp.exp2(neg_abs))
            if diagonal:
                drop = jnp.where(causal, drop, 0.0)
            drops.append(drop)
        incl = []
        for g in range(gh):
            hi = drops[g].astype(BF16)
            lo = (drops[g] - hi.astype(F32)).astype(BF16)
            incl.append(jnp.dot(hi, from_here, preferred_element_type=F32)
                        + jnp.dot(lo, from_here, preferred_element_type=F32))
        w = []
        for g in range(gh):
            wg = jnp.exp2(qk[g] - carry[g][0] - incl[g])
            if diagonal:
                wg = jnp.where(causal, wg, 0.0)
            w.append(wg.astype(BF16))
        pv = [jnp.dot(w[g], v_ref[pl.ds(off, kb), _head(g)], preferred_element_type=F32) for g in range(gh)]
        return tuple((carry[g][0] + incl[g][:, :1], carry[g][1] + pv[g]) for g in range(gh))

    init = tuple((jnp.zeros((tq, 1), F32), jnp.zeros((tq, HEAD_DIM), F32)) for _ in range(gh))
    first = block(pl.multiple_of((nkb - 1) * kb, kb), init, True)
    res = lax.fori_loop(1, nkb, lambda t, c: block(pl.multiple_of((nkb - 1 - t) * kb, kb), c, False), first)
    for g in range(gh):
        o_ref[:, _head(g)] = _gated(res[g][1], g_ref[:, _head(g)]).astype(o_ref.dtype)


def _band_kernel(q_ref, k_ref, v_ref, tab_ref, g_ref, o_ref, *, q0, sk_valid, gh):
    i = pl.program_id(2)
    tq = q_ref.shape[0]
    jabs = (q0 + i * tq) // TQ
    back = BAND_BLOCKS - 1
    shift = back - jnp.minimum(jabs, back)
    s0 = pl.multiple_of(jnp.maximum(jabs - back, 0) * TQ, TQ)
    width = BAND_BLOCKS * TQ
    qpos = q0 + i * tq + lax.broadcasted_iota(jnp.int32, (tq, width), 0)
    kpos = s0 + lax.broadcasted_iota(jnp.int32, (tq, width), 1)
    qc, kc = qpos >> CHUNK_SHIFT, kpos >> CHUNK_SHIFT
    allowed = (kc <= qc) & (kc >= qc - C_PAST_CHUNKS) & (kpos < sk_valid)
    qk = [_nt_dot(q_ref[:, _head(g)], k_ref[pl.ds(s0, width), _head(g)]) for g in range(gh)]
    entry = [jnp.clip(r + shift - (back - 2), 0, 2) for r in range(BAND_BLOCKS)]
    soft = []
    for g in range(gh):
        bias = jnp.concatenate([tab_ref[g, entry[r], :tq, :] for r in range(BAND_BLOCKS)], axis=1)
        s = jnp.where(allowed, qk[g] + bias, NEG_INF)
        p = jnp.exp2(s - s.max(axis=1, keepdims=True))
        soft.append((p.sum(axis=1, keepdims=True), p.astype(BF16)))
    pv = [jnp.dot(soft[g][1], v_ref[pl.ds(s0, width), _head(g)], preferred_element_type=F32)
          for g in range(gh)]
    for g in range(gh):
        o_ref[:, _head(g)] = _gated(pv[g] / soft[g][0], g_ref[:, _head(g)]).astype(o_ref.dtype)


def _attention(kind, q, kv, gate, extra, *, b, sq, sk, nh, q0, sk_valid=None, kb=None, gh=8, head0=0):
    tq = min(TQ if kind == "band" else 2 * TQ, sq)
    nqt = sq // tq
    assert nh % gh == 0 and sq % tq == 0 and q0 % TQ == 0 and tq % 16 == 0
    ng = nh // gh
    gw = gh * HEAD_DIM
    assert head0 % gh == 0
    q_spec = pl.BlockSpec((tq, gw), lambda bi, h, i: (bi * nqt + i, head0 // gh + h))
    o_spec = pl.BlockSpec((tq, gw), lambda bi, h, i: (bi * nqt + i, h))
    k_spec = pl.BlockSpec((sk, gw), lambda bi, h, i: (bi, h))
    v_spec = pl.BlockSpec((sk, gw), lambda bi, h, i: (bi, ng + h))
    head_vec = pl.BlockSpec((gh, 1, LANE), lambda bi, h, i: (h, 0, 0))
    if kind == "dsa":
        mask, slopes = extra
        nkbt = sk // kb
        body = functools.partial(_dsa_attn_kernel, q0=q0, kb=kb, nkbt=nkbt, gh=gh)
        extra_specs = [pl.BlockSpec((1, max(tq // TQ, 1), nkbt, TQ, kb), lambda bi, h, i: (bi, i, 0, 0, 0)),
                       head_vec]
        extra_args = [mask, slopes]
    elif kind == "stick":
        assert kb % tq == 0 and q0 % tq == 0
        body = functools.partial(_stick_kernel, q0=q0, kb=kb, nkbt=sk // kb, gh=gh)
        extra_specs, extra_args = [], []
    else:
        body = functools.partial(_band_kernel, q0=q0, sk_valid=sk_valid, gh=gh)
        extra_specs = [pl.BlockSpec((gh, 3, TQ, TQ), lambda bi, h, i: (h, 0, 0, 0))]
        extra_args = [extra]
    return pl.pallas_call(
        body,
        grid=(b, ng, nqt),
        in_specs=[q_spec, k_spec, v_spec] + extra_specs + [q_spec],
        out_specs=o_spec,
        out_shape=jax.ShapeDtypeStruct((b * sq, nh * HEAD_DIM), BF16),
        compiler_params=_cparams(("parallel", "parallel", "arbitrary")),
        name=kind + "_attention",
    )(q, kv, kv, *extra_args, gate)


def _pad_q(a, b, nq, sq):
    if nq == sq:
        return a
    return jnp.pad(a.reshape(b, nq, -1), ((0, 0), (0, sq - nq), (0, 0))).reshape(b * sq, -1)


def _with_cache(cache, new, b, nq, tail, lead):
    width = new.shape[-1]
    parts = [jnp.zeros((b, lead, width), BF16)] if lead else []
    parts += [cache.reshape(b, cache.shape[1], width).astype(BF16), new.reshape(b, nq, width).astype(BF16),
              jnp.zeros((b, tail - nq, width), BF16)]
    return jnp.concatenate(parts, axis=1).reshape(-1, width)


def _kv_rows_kernel(c_ref, n_ref, o_ref, *, lead_blocks, cache_blocks, nq, nsub):
    r = pl.program_id(1)
    rows, width = o_ref.shape[1], o_ref.shape[2]

    @pl.when(r < lead_blocks)
    def _():
        o_ref[...] = jnp.zeros(o_ref.shape, o_ref.dtype)

    @pl.when((r >= lead_blocks) & (r < lead_blocks + cache_blocks))
    def _():
        by_sub = jnp.swapaxes(c_ref[...], 0, 1)
        for c in range(nsub):
            o_ref[0, :, c * LANE:(c + 1) * LANE] = by_sub[c].astype(o_ref.dtype)

    @pl.when(r == lead_blocks + cache_blocks)
    def _():
        o_ref[0, :nq, :] = n_ref[0]
        o_ref[0, nq:, :] = jnp.zeros((rows - nq, width), o_ref.dtype)


def _kv_rows(cache, new, *, b, nq, lead):
    width = new.shape[-1]
    p = cache.shape[1]
    nsub = width // LANE
    assert p % KV_ROWS == 0 and lead % KV_ROWS == 0 and nq % 16 == 0 and nq <= KV_ROWS
    lead_blocks, cache_blocks = lead // KV_ROWS, p // KV_ROWS
    cache_block = lambda bi, r: (bi * cache_blocks + jnp.clip(r - lead_blocks, 0, cache_blocks - 1), 0, 0)
    return pl.pallas_call(
        functools.partial(_kv_rows_kernel, lead_blocks=lead_blocks, cache_blocks=cache_blocks, nq=nq, nsub=nsub),
        grid=(b, lead_blocks + cache_blocks + 1),
        in_specs=[pl.BlockSpec((KV_ROWS, nsub, LANE), cache_block),
                  pl.BlockSpec((1, nq, width), lambda bi, r: (bi, 0, 0))],
        out_specs=pl.BlockSpec((1, KV_ROWS, width), lambda bi, r: (bi, r, 0)),
        out_shape=jax.ShapeDtypeStruct((b, lead + p + KV_ROWS, width), BF16),
        compiler_params=_cparams(("parallel", "arbitrary")),
        name="kv_rows",
    )(cache.reshape(b * p, nsub, LANE), new.reshape(b, nq, width)).reshape(-1, width)


def _even_layer(x, wts, cache, *, b, nq, q0):
    sq = -(-nq // TQ) * TQ
    na = wts["out_a"].shape[0] // HEAD_DIM
    nb = wts["out_b"].shape[0] // HEAD_DIM
    h = _rmsnorm(x, wts["norm"], BF16)
    q_ab = _mm([(h, wts["q_ab"])], [BF16], scale=Q_SCALE)
    g_ab = _mm([(h, wts["g_ab"])], [F32])
    kva32, kva16 = _mm([(h, wts["kva"])], [F32, BF16])
    kvb32, kvb16 = _mm([(h, wts["kvb"])], [F32, BF16])
    idx32, idx16 = _mm([(h, wts["idx"])], [F32, BF16], tm=512, tn=wts["idx"].shape[1])

    n_iq = N_IDX_HEADS * IDX_DIM
    ikn = _iknorm(idx32, wts["gk"], n_iq // LANE)
    iw = idx32[:, n_iq + IDX_DIM:n_iq + IDX_DIM + N_IDX_HEADS]
    iwt = _pad_q(iw, b, nq, sq).T

    if cache is None:
        sk = sk_valid = sq
        ka_all, kb_all = kva16, kvb16
        ik_all = ikn.reshape(b, sk, IDX_DIM).astype(BF16)
    else:
        a_kv, a_kidx, b_kv = cache
        sk, sk_valid = q0 + KV_ROWS, q0 + nq
        ka_all = _kv_rows(a_kv, kva16, b=b, nq=nq, lead=0)
        kb_all = _kv_rows(b_kv, kvb16, b=b, nq=nq, lead=0)
        ik_all = _with_cache(a_kidx, ikn, b, nq, KV_ROWS, 0).reshape(b, sk, IDX_DIM)
    zeros = jnp.zeros_like(ik_all)
    ik2 = jnp.concatenate([ik_all, zeros, zeros, ik_all], axis=-1).reshape(b * sk, 2 * LANE)

    pad = functools.partial(_pad_q, b=b, nq=nq, sq=sq)
    kb_d = _pick(sk, (512, 384, 256, 128))
    mask = _dsa_mask(pad(idx16), ik2, iwt, b=b, sq=sq, sk=sk, q0=q0, sk_valid=sk_valid,
                     topk=min(TOPK_MAX, sk_valid // 4), kb=kb_d)
    slopes = 2.0 ** (-8.0 * jnp.arange(1, na + 1, dtype=F32) / na)
    slopes = jnp.broadcast_to(slopes[:, None, None], (na, 1, LANE))
    oa = _attention("dsa", q_ab, ka_all, g_ab, (mask, slopes), b=b, sq=nq, sk=sk, nh=na, q0=q0, kb=kb_d, gh=4)
    ob = _attention("stick", q_ab, kb_all, g_ab, None, b=b, sq=nq, sk=sk, nh=nb, q0=q0,
                    kb=_pick(sk, (256, 384, 128)), head0=na)
    x_new = _mm([(oa, wts["out_a"]), (ob, wts["out_b"])], [F32], res=x)
    return x_new, kva32, ikn, kvb32


def _odd_layer(x, wts, cache, *, b, nq, q0):
    sq = -(-nq // TQ) * TQ
    nc = wts["q"].shape[1] // HEAD_DIM
    h = _rmsnorm(x, wts["norm"], BF16)
    q = _mm([(h, wts["q"])], [BF16], scale=Q_SCALE)
    kv32, kv16 = _mm([(h, wts["kv"])], [F32, BF16])
    g = _mm([(h, wts["g"])], [F32])
    if cache is None:
        sk = sk_valid = sq
        kv_all = kv16
    else:
        sk, sk_valid = q0 + KV_ROWS, q0 + nq
        kv_all = _kv_rows(cache, kv16, b=b, nq=nq, lead=q0 - cache.shape[1])
    o = _attention("band", q, kv_all, g, wts["tab"], b=b, sq=nq, sk=sk, nh=nc, q0=q0, sk_valid=sk_valid)
    x_new = _mm([(o, wts["out"])], [F32], res=x)
    return x_new, kv32


def _even_weights(norm, w_in, gk, w_out, d):
    wa = d // 2
    n_idx = N_IDX_HEADS * IDX_DIM + IDX_DIM + N_IDX_HEADS
    n_idx_pad = -(-n_idx // LANE) * LANE
    o_idx = 4 * wa
    o_b = o_idx + n_idx
    cut = lambda lo, n: w_in[:, lo:lo + n].astype(BF16)
    return {
        "norm": norm, "gk": gk,
        "q_ab": jnp.concatenate([cut(0, wa), cut(o_b, wa)], axis=1),
        "g_ab": jnp.concatenate([cut(3 * wa, wa), cut(o_b + 3 * wa, wa)], axis=1),
        "kva": cut(wa, 2 * wa), "kvb": cut(o_b + wa, 2 * wa),
        "idx": jnp.pad(cut(o_idx, n_idx), ((0, 0), (0, n_idx_pad - n_idx))),
        "out_a": w_out[:wa].astype(BF16), "out_b": w_out[wa:].astype(BF16),
    }


def _odd_weights(norm, w_in, rel_bias, w_out, d):
    nc = rel_bias.shape[0]
    rb = rel_bias * LOG2E
    lo, hi = rb[:, :1], rb[:, 2 * REL_CLIP:]
    by_dist = jnp.concatenate([jnp.broadcast_to(hi, (nc, 2 * TQ - 1 - REL_CLIP)), rb[:, ::-1],
                               jnp.broadcast_to(lo, (nc, TQ - 1 - REL_CLIP))], axis=1)
    window = lambda base: jnp.stack([by_dist[:, 2 * TQ - 1 - base - t:3 * TQ - 1 - base - t]
                                     for t in range(TQ)], axis=1)
    far = jnp.broadcast_to(hi[:, :, None], (nc, TQ, TQ))
    return {
        "norm": norm,
        "q": w_in[:, :d].astype(BF16), "kv": w_in[:, d:3 * d].astype(BF16), "g": w_in[:, 3 * d:].astype(BF16),
        "tab": jnp.stack([far, window(TQ), window(0)], axis=1),
        "out": w_out.astype(BF16),
    }


def kernel(x_prompt, x_sample, cache_a_kv, cache_a_kidx, cache_b_kv, cache_c_kv, norm_e, w_in_e,
           idx_k_gain, w_out_e, norm_o, w_in_o, rel_bias_o, w_out_o, norm_f):
    bp, sp, d = x_prompt.shape
    bs, ts, _ = x_sample.shape
    p_len = cache_a_kv.shape[2]
    depth = norm_e.shape[0] + norm_o.shape[0]
    assert sp % TQ == 0 and p_len % TQ == 0 and ts <= TQ
    assert REL_CLIP <= CHUNK and TQ == 2 * CHUNK

    xp = x_prompt.reshape(bp * sp, d)
    xs = x_sample.reshape(bs * ts, d)
    rows_p = lambda a, *tail: a.reshape(bp, sp, *tail)
    rows_s = lambda a, *tail: a.reshape(bs, ts, *tail)
    outs = {k: [] for k in ("a_kv_p", "a_ix_p", "b_kv_p", "c_kv_p", "a_kv_s", "a_ix_s", "b_kv_s", "c_kv_s")}
    for layer in range(depth):
        j = layer // 2
        if layer % 2 == 0:
            wts = _even_weights(norm_e[j], w_in_e[j], idx_k_gain[j], w_out_e[j], d)
            na = wts["out_a"].shape[0] // HEAD_DIM
            nb = wts["out_b"].shape[0] // HEAD_DIM
            xp, akv, aix, bkv = _even_layer(xp, wts, None, b=bp, nq=sp, q0=0)
            xs, akv2, aix2, bkv2 = _even_layer(xs, wts, (cache_a_kv[j], cache_a_kidx[j], cache_b_kv[j]),
                                               b=bs, nq=ts, q0=p_len)
            outs["a_kv_p"].append(rows_p(akv, 2, na, HEAD_DIM))
            outs["a_ix_p"].append(rows_p(aix, IDX_DIM))
            outs["b_kv_p"].append(rows_p(bkv, 2, nb, HEAD_DIM))
            outs["a_kv_s"].append(rows_s(akv2, 2, na, HEAD_DIM))
            outs["a_ix_s"].append(rows_s(aix2, IDX_DIM))
            outs["b_kv_s"].append(rows_s(bkv2, 2, nb, HEAD_DIM))
        else:
            wts = _odd_weights(norm_o[j], w_in_o[j], rel_bias_o[j], w_out_o[j], d)
            nc = wts["q"].shape[1] // HEAD_DIM
            wc = min(C_PAST_CHUNKS * CHUNK, sp)
            xp, ckv = _odd_layer(xp, wts, None, b=bp, nq=sp, q0=0)
            xs, ckv2 = _odd_layer(xs, wts, cache_c_kv[j], b=bs, nq=ts, q0=p_len)
            outs["c_kv_p"].append(rows_p(ckv, 2, nc, HEAD_DIM)[:, sp - wc:])
            outs["c_kv_s"].append(rows_s(ckv2, 2, nc, HEAD_DIM))
    y_prompt = _rmsnorm(xp, norm_f, F32).reshape(bp, sp, d)
    y_sample = _rmsnorm(xs, norm_f, F32).reshape(bs, ts, d)
    return (y_prompt, y_sample,
            jnp.stack(outs["a_kv_p"]), jnp.stack(outs["a_ix_p"]), jnp.stack(outs["b_kv_p"]), jnp.stack(outs["c_kv_p"]),
            jnp.stack(outs["a_kv_s"]), jnp.stack(outs["a_ix_s"]), jnp.stack(outs["b_kv_s"]), jnp.stack(outs["c_kv_s"]))
```

```python
import functools

import jax
import jax.numpy as jnp
from jax import lax
from jax.experimental import pallas as pl
from jax.experimental.pallas import tpu as pltpu

F32 = jnp.float32
BF16 = jnp.bfloat16

HEAD_DIM = 128
CHUNK = 64
CHUNK_SHIFT = 6
N_IDX_HEADS = 16
IDX_DIM = 64
TOPK_MAX = 256
C_PAST_CHUNKS = 8
REL_CLIP = 64
EPS = 1e-6
ATT_SCALE = HEAD_DIM ** -0.5
LOG2E = 1.4426950408889634
Q_SCALE = ATT_SCALE * LOG2E
IDX_SCALE = (IDX_DIM ** -0.5) * (N_IDX_HEADS ** -0.5)

TQ = 128
LANE = 128
BAND_BLOCKS = (C_PAST_CHUNKS * CHUNK) // TQ + 1
INT_MIN = -2 ** 31
NEG_INF = float("-inf")
VMEM_LIMIT = 56 * 1024 * 1024


def _cparams(sem):
    return pltpu.CompilerParams(dimension_semantics=sem, vmem_limit_bytes=VMEM_LIMIT)


def _pick(n, candidates):
    for c in candidates:
        if n % c == 0:
            return c
    raise ValueError(f"no block size in {candidates} divides {n}")


def _rmsnorm_kernel(x_ref, g_ref, o_ref):
    x = x_ref[...]
    ms = jnp.mean(x * x, axis=-1, keepdims=True)
    o_ref[...] = (x * lax.rsqrt(ms + EPS) * g_ref[...]).astype(o_ref.dtype)


def _rmsnorm(x, g, out_dtype):
    m, d = x.shape
    tm = _pick(m, (256, 128, 64, 32, 16, 8))
    return pl.pallas_call(
        _rmsnorm_kernel,
        grid=(m // tm,),
        in_specs=[pl.BlockSpec((tm, d), lambda i: (i, 0)),
                  pl.BlockSpec((1, d), lambda i: (0, 0))],
        out_specs=pl.BlockSpec((tm, d), lambda i: (i, 0)),
        out_shape=jax.ShapeDtypeStruct((m, d), out_dtype),
        compiler_params=_cparams(("parallel",)),
        name="rmsnorm",
    )(x, g.reshape(1, d))


def _cast_kernel(x_ref, o_ref):
    o_ref[...] = x_ref[...].astype(o_ref.dtype)


def _to_bf16(w):
    k, n = w.shape
    tk = _pick(k, (128, 64, 32, 16))
    return pl.pallas_call(
        _cast_kernel,
        grid=(k // tk,),
        in_specs=[pl.BlockSpec((tk, n), lambda i: (i, 0))],
        out_specs=pl.BlockSpec((tk, n), lambda i: (i, 0)),
        out_shape=jax.ShapeDtypeStruct((k, n), BF16),
        compiler_params=_cparams(("parallel",)),
        name="to_bf16",
    )(w)


def _iknorm_kernel(x_ref, g_ref, o_ref):
    x = x_ref[:, :IDX_DIM]
    ms = jnp.mean(x * x, axis=-1, keepdims=True)
    o_ref[...] = x * lax.rsqrt(ms + EPS) * g_ref[...]


def _iknorm(idx32, gk, col_block):
    m = idx32.shape[0]
    tm = _pick(m, (1024, 512, 256, 128, 64, 32, 16, 8))
    return pl.pallas_call(
        _iknorm_kernel,
        grid=(m // tm,),
        in_specs=[pl.BlockSpec((tm, LANE), lambda i: (i, col_block)),
                  pl.BlockSpec((1, IDX_DIM), lambda i: (0, 0))],
        out_specs=pl.BlockSpec((tm, IDX_DIM), lambda i: (i, 0)),
        out_shape=jax.ShapeDtypeStruct((m, IDX_DIM), F32),
        compiler_params=_cparams(("parallel",)),
        name="iknorm",
    )(idx32, gk.reshape(1, IDX_DIM))


def _mm_kernel(*refs, n_pairs, has_res, n_out, scale):
    acc = None
    for p in range(n_pairs):
        d = jnp.dot(refs[2 * p][...], refs[2 * p + 1][...], preferred_element_type=F32)
        acc = d if acc is None else acc + d
    pos = 2 * n_pairs
    if has_res:
        acc = acc + refs[pos][...]
        pos += 1
    if scale is not None:
        acc = acc * scale
    for o_ref in refs[pos:pos + n_out]:
        o_ref[...] = acc.astype(o_ref.dtype)


def _cols(w, lo, n):
    return (w, lo, n)


def _mm(pairs, out_dtypes, res=None, scale=None, tm=1024, tn=1024):
    pairs = [(a, w if isinstance(w, tuple) else (w, 0, w.shape[1])) for a, w in pairs]
    m = pairs[0][0].shape[0]
    n = pairs[0][1][2]
    tm = min(tm, m)
    tn = min(tn, n)
    assert m % tm == 0 and n % tn == 0, (m, n, tm, tn)
    in_specs, args = [], []
    for a, (w, lo, width) in pairs:
        k = a.shape[1]
        assert width == n and lo % tn == 0, (lo, width, tn)
        in_specs += [pl.BlockSpec((tm, k), lambda i, j: (i, 0)),
                     pl.BlockSpec((k, tn), lambda i, j, j0=lo // tn: (0, j0 + j))]
        args += [a, w]
    if res is not None:
        in_specs.append(pl.BlockSpec((tm, tn), lambda i, j: (i, j)))
        args.append(res)
    outs = pl.pallas_call(
        functools.partial(_mm_kernel, n_pairs=len(pairs), has_res=res is not None,
                          n_out=len(out_dtypes), scale=scale),
        grid=(m // tm, n // tn),
        in_specs=in_specs,
        out_specs=[pl.BlockSpec((tm, tn), lambda i, j: (i, j)) for _ in out_dtypes],
        out_shape=[jax.ShapeDtypeStruct((m, n), dt) for dt in out_dtypes],
        compiler_params=_cparams(("parallel", "arbitrary")),
        name="matmul",
    )(*args)
    return outs[0] if len(out_dtypes) == 1 else tuple(outs)


def _nt_dot(a, b):
    return lax.dot_general(a, b, (((1,), (1,)), ((), ())), preferred_element_type=F32)


def _dsa_mask_kernel(iq_ref, ik_ref, iwt_ref, o_ref, keys_ref, *, q0, kb, nkbt, sk_valid, topk):
    i = pl.program_id(1)
    nkb = jnp.minimum(nkbt, (q0 + (i + 1) * TQ + kb - 1) // kb)
    w = iwt_ref[...] * IDX_SCALE
    qchunk = (q0 + i * TQ + lax.broadcasted_iota(jnp.int32, (1, TQ), 1)) >> CHUNK_SHIFT

    def build(t, _):
        off = pl.multiple_of(t * kb, kb)
        acc = jnp.zeros((kb, TQ), F32)
        for h in range(N_IDX_HEADS):
            e, p = h % 2, h // 2
            ikh = ik_ref[pl.ds(off, kb), e * LANE:(e + 1) * LANE]
            s = _nt_dot(ikh, iq_ref[:, p * LANE:(p + 1) * LANE])
            acc = acc + jnp.maximum(s, 0.0) * w[h:h + 1, :]
        acc = acc + 0.0
        bits = lax.bitcast_convert_type(acc, jnp.int32)
        key = jnp.where(bits < 0, bits ^ 0x7FFFFFFF, bits)
        kpos = off + lax.broadcasted_iota(jnp.int32, (kb, TQ), 0)
        adm = ((kpos >> CHUNK_SHIFT) <= qchunk) & (kpos < sk_valid)
        keys_ref[pl.ds(off, kb), :] = jnp.where(adm, key, INT_MIN)
        return 0

    lax.fori_loop(0, nkb, build, 0)

    def count(pred):
        def body(t, acc8):
            off = pl.multiple_of(t * kb, kb)
            hit = pred(keys_ref[pl.ds(off, kb), :]).astype(jnp.int32)
            return acc8 + hit.reshape(kb // 8, 8, TQ).sum(axis=0)
        acc8 = lax.fori_loop(0, nkb, body, jnp.zeros((8, TQ), jnp.int32))
        return acc8.sum(axis=0, keepdims=True)

    thr = jnp.where(count(lambda kx: kx >= 0) >= topk, 0, INT_MIN).astype(jnp.int32)

    def bit_step(t, thr):
        cand = thr | (jnp.int32(1) << (30 - t))
        return jnp.where(count(lambda kx: kx >= cand) >= topk, cand, thr)

    thr = lax.fori_loop(0, 31, bit_step, thr)
    n_ge = count(lambda kx: kx >= thr)
    tie = jnp.max(jnp.where(thr == INT_MIN, 0, n_ge)) > topk

    @pl.when(jnp.logical_not(tie))
    def _():
        def emit(t, _):
            off = pl.multiple_of(t * kb, kb)
            kx = keys_ref[pl.ds(off, kb), :]
            sel = (kx >= thr) & (kx != INT_MIN)
            o_ref[0, 0, t] = jnp.where(sel, 0.0, NEG_INF).T.astype(o_ref.dtype)
            return 0

        lax.fori_loop(0, nkb, emit, 0)

    @pl.when(tie)
    def _():
        need = (topk - count(lambda kx: kx > thr)).astype(F32)
        tri = (lax.broadcasted_iota(jnp.int32, (kb, kb), 0)
               >= lax.broadcasted_iota(jnp.int32, (kb, kb), 1)).astype(BF16)

        def emit(t, seen):
            off = pl.multiple_of(t * kb, kb)
            kx = keys_ref[pl.ds(off, kb), :]
            eq = kx == thr
            rank = jnp.dot(tri, eq.astype(BF16), preferred_element_type=F32) + seen
            sel = ((kx > thr) | (eq & (rank <= need))) & (kx != INT_MIN)
            o_ref[0, 0, t] = jnp.where(sel, 0.0, NEG_INF).T.astype(o_ref.dtype)
            return rank[kb - 1:kb, :]

        lax.fori_loop(0, nkb, emit, jnp.zeros((1, TQ), F32))

    def fill(t, _):
        o_ref[0, 0, t] = jnp.full((TQ, kb), NEG_INF, o_ref.dtype)
        return 0

    lax.fori_loop(nkb, nkbt, fill, 0)


def _dsa_mask(idx16, ik2, iwt, *, b, sq, sk, q0, sk_valid, topk, kb):
    nqt, nkbt = sq // TQ, sk // kb
    n_iq = N_IDX_HEADS * IDX_DIM
    return pl.pallas_call(
        functools.partial(_dsa_mask_kernel, q0=q0, kb=kb, nkbt=nkbt, sk_valid=sk_valid, topk=topk),
        grid=(b, nqt),
        in_specs=[pl.BlockSpec((TQ, n_iq), lambda bi, i: (bi * nqt + i, 0)),
                  pl.BlockSpec((sk, 2 * LANE), lambda bi, i: (bi, 0)),
                  pl.BlockSpec((N_IDX_HEADS, TQ), lambda bi, i: (0, bi * nqt + i))],
        out_specs=pl.BlockSpec((1, 1, nkbt, TQ, kb), lambda bi, i: (bi, i, 0, 0, 0)),
        out_shape=jax.ShapeDtypeStruct((b, nqt, nkbt, TQ, kb), BF16),
        scratch_shapes=[pltpu.VMEM((sk, TQ), jnp.int32)],
        compiler_params=_cparams(("parallel", "arbitrary")),
        name="dsa_mask",
    )(idx16, ik2, iwt)


def _gated(o, g):
    return o * (g * jax.nn.sigmoid(g))


def _head(g):
    return slice(g * HEAD_DIM, (g + 1) * HEAD_DIM)


def _dsa_attn_kernel(q_ref, k_ref, v_ref, m_ref, sl_ref, g_ref, o_ref, *, q0, kb, nkbt, gh):
    i = pl.program_id(2)
    tq = q_ref.shape[0]
    nkb = jnp.minimum(nkbt, (q0 + (i + 1) * tq + kb - 1) // kb)
    qpos = q0 + i * tq + lax.broadcasted_iota(jnp.int32, (tq, kb), 0)
    col = lax.broadcasted_iota(jnp.int32, (tq, kb), 1)

    def body(t, carry):
        off = pl.multiple_of(t * kb, kb)
        dist = jnp.abs(qpos - (off + col)).astype(F32)
        if tq > TQ:
            mask = jnp.concatenate([m_ref[0, u, t] for u in range(tq // TQ)], axis=0).astype(F32)
        else:
            mask = m_ref[0, 0, t, :tq, :].astype(F32)
        qk = [_nt_dot(q_ref[:, _head(g)], k_ref[pl.ds(off, kb), _head(g)]) for g in range(gh)]
        soft = []
        for g in range(gh):
            m, l, _ = carry[g]
            s = qk[g] - (sl_ref[g][:, :1] * LOG2E) * dist + mask
            m_new = jnp.maximum(m, s.max(axis=1, keepdims=True))
            m_use = jnp.where(m_new == NEG_INF, 0.0, m_new)
            alpha = jnp.exp2(m - m_use)
            p = jnp.exp2(s - m_use)
            soft.append((m_new, alpha * l + p.sum(axis=1, keepdims=True), alpha, p.astype(BF16)))
        pv = [jnp.dot(soft[g][3], v_ref[pl.ds(off, kb), _head(g)], preferred_element_type=F32)
              for g in range(gh)]
        return tuple((soft[g][0], soft[g][1], soft[g][2] * carry[g][2] + pv[g]) for g in range(gh))

    init = tuple((jnp.full((tq, 1), NEG_INF, F32), jnp.zeros((tq, 1), F32),
                  jnp.zeros((tq, HEAD_DIM), F32)) for _ in range(gh))
    res = lax.fori_loop(0, nkb, body, init)
    for g in range(gh):
        _, l, acc = res[g]
        o_ref[:, _head(g)] = _gated(acc / l, g_ref[:, _head(g)]).astype(o_ref.dtype)


def _stick_kernel(q_ref, k_ref, v_ref, g_ref, o_ref, *, q0, kb, nkbt, gh):
    i = pl.program_id(2)
    tq = q_ref.shape[0]
    nkb = jnp.minimum(nkbt, (q0 + (i + 1) * tq + kb - 1) // kb)
    from_here = (lax.broadcasted_iota(jnp.int32, (kb, kb), 0)
                 >= lax.broadcasted_iota(jnp.int32, (kb, kb), 1)).astype(BF16)
    qpos = q0 + i * tq + lax.broadcasted_iota(jnp.int32, (tq, kb), 0)
    col = lax.broadcasted_iota(jnp.int32, (tq, kb), 1)

    def block(off, carry, diagonal):
        causal = (off + col) < qpos
        qk = [_nt_dot(q_ref[:, _head(g)], k_ref[pl.ds(off, kb), _head(g)]) for g in range(gh)]
        drops = []
        for g in range(gh):
            z2 = qk[g]
            neg_abs = lax.bitcast_convert_type(lax.bitcast_convert_type(z2, jnp.int32) | INT_MIN, F32)
            drop = jnp.maximum(z2, 0.0) + jnp.log2(1.0 + jnp.exp2(neg_abs))
            if diagonal:
                drop = jnp.where(causal, drop, 0.0)
            drops.append(drop)
        incl = []
        for g in range(gh):
            hi = drops[g].astype(BF16)
            lo = (drops[g] - hi.astype(F32)).astype(BF16)
            incl.append(jnp.dot(hi, from_here, preferred_element_type=F32)
                        + jnp.dot(lo, from_here, preferred_element_type=F32))
        w = []
        for g in range(gh):
            wg = jnp.exp2(qk[g] - carry[g][0] - incl[g])
            if diagonal:
                wg = jnp.where(causal, wg, 0.0)
            w.append(wg.astype(BF16))
        pv = [jnp.dot(w[g], v_ref[pl.ds(off, kb), _head(g)], preferred_element_type=F32) for g in range(gh)]
        return tuple((carry[g][0] + incl[g][:, :1], carry[g][1] + pv[g]) for g in range(gh))

    init = tuple((jnp.zeros((tq, 1), F32), jnp.zeros((tq, HEAD_DIM), F32)) for _ in range(gh))
    first = block(pl.multiple_of((nkb - 1) * kb, kb), init, True)
    res = lax.fori_loop(1, nkb, lambda t, c: block(pl.multiple_of((nkb - 1 - t) * kb, kb), c, False), first)
    for g in range(gh):
        o_ref[:, _head(g)] = _gated(res[g][1], g_ref[:, _head(g)]).astype(o_ref.dtype)


def _band_kernel(q_ref, k_ref, v_ref, tab_ref, g_ref, o_ref, *, q0, sk_valid, gh):
    i = pl.program_id(2)
    tq = q_ref.shape[0]
    jabs = (q0 + i * tq) // TQ
    back = BAND_BLOCKS - 1
    shift = back - jnp.minimum(jabs, back)
    s0 = pl.multiple_of(jnp.maximum(jabs - back, 0) * TQ, TQ)
    width = BAND_BLOCKS * TQ
    qpos = q0 + i * tq + lax.broadcasted_iota(jnp.int32, (tq, width), 0)
    kpos = s0 + lax.broadcasted_iota(jnp.int32, (tq, width), 1)
    qc, kc = qpos >> CHUNK_SHIFT, kpos >> CHUNK_SHIFT
    allowed = (kc <= qc) & (kc >= qc - C_PAST_CHUNKS) & (kpos < sk_valid)
    qk = [_nt_dot(q_ref[:, _head(g)], k_ref[pl.ds(s0, width), _head(g)]) for g in range(gh)]
    entry = [jnp.clip(r + shift - (back - 2), 0, 2) for r in range(BAND_BLOCKS)]
    soft = []
    for g in range(gh):
        bias = jnp.concatenate([tab_ref[g, entry[r], :tq, :] for r in range(BAND_BLOCKS)], axis=1)
        s = jnp.where(allowed, qk[g] + bias, NEG_INF)
        p = jnp.exp2(s - s.max(axis=1, keepdims=True))
        soft.append((p.sum(axis=1, keepdims=True), p.astype(BF16)))
    pv = [jnp.dot(soft[g][1], v_ref[pl.ds(s0, width), _head(g)], preferred_element_type=F32)
          for g in range(gh)]
    for g in range(gh):
        o_ref[:, _head(g)] = _gated(pv[g] / soft[g][0], g_ref[:, _head(g)]).astype(o_ref.dtype)


def _attention(kind, q, kv, gate, extra, *, b, sq, sk, nh, q0, sk_valid=None, kb=None, gh=8):
    tq = min(TQ if kind == "band" else 2 * TQ, sq)
    nqt = sq // tq
    assert nh % gh == 0 and sq % tq == 0 and q0 % TQ == 0 and tq % 16 == 0
    ng = nh // gh
    gw = gh * HEAD_DIM
    q_spec = pl.BlockSpec((tq, gw), lambda bi, h, i: (bi * nqt + i, h))
    k_spec = pl.BlockSpec((sk, gw), lambda bi, h, i: (bi, h))
    v_spec = pl.BlockSpec((sk, gw), lambda bi, h, i: (bi, ng + h))
    head_vec = pl.BlockSpec((gh, 1, LANE), lambda bi, h, i: (h, 0, 0))
    if kind == "dsa":
        mask, slopes = extra
        nkbt = sk // kb
        body = functools.partial(_dsa_attn_kernel, q0=q0, kb=kb, nkbt=nkbt, gh=gh)
        extra_specs = [pl.BlockSpec((1, max(tq // TQ, 1), nkbt, TQ, kb), lambda bi, h, i: (bi, i, 0, 0, 0)),
                       head_vec]
        extra_args = [mask, slopes]
    elif kind == "stick":
        assert kb % tq == 0 and q0 % tq == 0
        body = functools.partial(_stick_kernel, q0=q0, kb=kb, nkbt=sk // kb, gh=gh)
        extra_specs, extra_args = [], []
    else:
        body = functools.partial(_band_kernel, q0=q0, sk_valid=sk_valid, gh=gh)
        extra_specs = [pl.BlockSpec((gh, 3, TQ, TQ), lambda bi, h, i: (h, 0, 0, 0))]
        extra_args = [extra]
    return pl.pallas_call(
        body,
        grid=(b, ng, nqt),
        in_specs=[q_spec, k_spec, v_spec] + extra_specs + [q_spec],
        out_specs=q_spec,
        out_shape=jax.ShapeDtypeStruct((b * sq, nh * HEAD_DIM), BF16),
        compiler_params=_cparams(("parallel", "parallel", "arbitrary")),
        name=kind + "_attention",
    )(q, kv, kv, *extra_args, gate)


def _pad_q(a, b, nq, sq):
    if nq == sq:
        return a
    return jnp.pad(a.reshape(b, nq, -1), ((0, 0), (0, sq - nq), (0, 0))).reshape(b * sq, -1)


def _with_cache(cache, new, b, nq, sq, lead):
    width = new.shape[-1]
    parts = [jnp.zeros((b, lead, width), BF16)] if lead else []
    parts += [cache.reshape(b, cache.shape[1], width).astype(BF16), new.reshape(b, nq, width).astype(BF16),
              jnp.zeros((b, sq - nq, width), BF16)]
    return jnp.concatenate(parts, axis=1).reshape(-1, width)


def _kv_rows_kernel(c_ref, n_ref, o_ref, *, lead_blocks, cache_blocks, nq, nsub):
    r = pl.program_id(1)
    rows, width = o_ref.shape[1], o_ref.shape[2]

    @pl.when(r < lead_blocks)
    def _():
        o_ref[...] = jnp.zeros(o_ref.shape, o_ref.dtype)

    @pl.when((r >= lead_blocks) & (r < lead_blocks + cache_blocks))
    def _():
        by_sub = jnp.swapaxes(c_ref[...], 0, 1)
        for c in range(nsub):
            o_ref[0, :, c * LANE:(c + 1) * LANE] = by_sub[c].astype(o_ref.dtype)

    @pl.when(r == lead_blocks + cache_blocks)
    def _():
        o_ref[0, :nq, :] = n_ref[0]
        o_ref[0, nq:, :] = jnp.zeros((rows - nq, width), o_ref.dtype)


def _kv_rows(cache, new, *, b, nq, sq, lead):
    width = new.shape[-1]
    p = cache.shape[1]
    nsub = width // LANE
    assert sq == TQ and p % TQ == 0 and lead % TQ == 0 and nq % 16 == 0
    lead_blocks, cache_blocks = lead // TQ, p // TQ
    cache_block = lambda bi, r: (bi * cache_blocks + jnp.clip(r - lead_blocks, 0, cache_blocks - 1), 0, 0)
    return pl.pallas_call(
        functools.partial(_kv_rows_kernel, lead_blocks=lead_blocks, cache_blocks=cache_blocks, nq=nq, nsub=nsub),
        grid=(b, lead_blocks + cache_blocks + 1),
        in_specs=[pl.BlockSpec((TQ, nsub, LANE), cache_block),
                  pl.BlockSpec((1, nq, width), lambda bi, r: (bi, 0, 0))],
        out_specs=pl.BlockSpec((1, TQ, width), lambda bi, r: (bi, r, 0)),
        out_shape=jax.ShapeDtypeStruct((b, lead + p + sq, width), BF16),
        compiler_params=_cparams(("parallel", "arbitrary")),
        name="kv_rows",
    )(cache.reshape(b * p, nsub, LANE), new.reshape(b, nq, width)).reshape(-1, width)


def _even_layer(x, wts, cache, *, b, nq, q0):
    sq = -(-nq // TQ) * TQ
    na = wts["out_a"].shape[0] // HEAD_DIM
    nb = wts["out_b"].shape[0] // HEAD_DIM
    h = _rmsnorm(x, wts["norm"], BF16)
    qa = _mm([(h, wts["qa"])], [BF16], scale=Q_SCALE)
    kva32, kva16 = _mm([(h, wts["kva"])], [F32, BF16])
    ga = _mm([(h, wts["ga"])], [F32])
    idx32, idx16 = _mm([(h, wts["idx"])], [F32, BF16], tm=512, tn=wts["idx"].shape[1])
    qb = _mm([(h, wts["qb"])], [BF16], scale=Q_SCALE)
    kvb32, kvb16 = _mm([(h, wts["kvb"])], [F32, BF16])
    gb = _mm([(h, wts["gb"])], [F32])

    n_iq = N_IDX_HEADS * IDX_DIM
    ikn = _iknorm(idx32, wts["gk"], n_iq // LANE)
    iw = idx32[:, n_iq + IDX_DIM:n_iq + IDX_DIM + N_IDX_HEADS]
    iwt = _pad_q(iw, b, nq, sq).T

    if cache is None:
        sk = sk_valid = sq
        ka_all, kb_all = kva16, kvb16
        ik_all = ikn.reshape(b, sk, IDX_DIM).astype(BF16)
    else:
        a_kv, a_kidx, b_kv = cache
        sk, sk_valid = q0 + sq, q0 + nq
        ka_all = _kv_rows(a_kv, kva16, b=b, nq=nq, sq=sq, lead=0)
        kb_all = _kv_rows(b_kv, kvb16, b=b, nq=nq, sq=sq, lead=0)
        ik_all = _with_cache(a_kidx, ikn, b, nq, sq, 0).reshape(b, sk, IDX_DIM)
    zeros = jnp.zeros_like(ik_all)
    ik2 = jnp.concatenate([ik_all, zeros, zeros, ik_all], axis=-1).reshape(b * sk, 2 * LANE)

    pad = functools.partial(_pad_q, b=b, nq=nq, sq=sq)
    kb_d = _pick(sk, (512, 384, 256, 128))
    mask = _dsa_mask(pad(idx16), ik2, iwt, b=b, sq=sq, sk=sk, q0=q0, sk_valid=sk_valid,
                     topk=min(TOPK_MAX, sk_valid // 4), kb=kb_d)
    slopes = 2.0 ** (-8.0 * jnp.arange(1, na + 1, dtype=F32) / na)
    slopes = jnp.broadcast_to(slopes[:, None, None], (na, 1, LANE))
    oa = _attention("dsa", qa, ka_all, ga, (mask, slopes), b=b, sq=nq, sk=sk, nh=na, q0=q0, kb=kb_d, gh=4)
    ob = _attention("stick", qb, kb_all, gb, None, b=b, sq=nq, sk=sk, nh=nb, q0=q0,
                    kb=_pick(sk, (256, 384, 128)))
    x_new = _mm([(oa, wts["out_a"]), (ob, wts["out_b"])], [F32], res=x)
    return x_new, kva32, ikn, kvb32


def _odd_layer(x, wts, cache, *, b, nq, q0):
    sq = -(-nq // TQ) * TQ
    nc = wts["out"].shape[0] // HEAD_DIM
    h = _rmsnorm(x, wts["norm"], BF16)
    q = _mm([(h, wts["q"])], [BF16], scale=Q_SCALE)
    kv32, kv16 = _mm([(h, wts["kv"])], [F32, BF16])
    g = _mm([(h, wts["g"])], [F32])
    if cache is None:
        sk = sk_valid = sq
        kv_all = kv16
    else:
        sk, sk_valid = q0 + sq, q0 + nq
        kv_all = _kv_rows(cache, kv16, b=b, nq=nq, sq=sq, lead=q0 - cache.shape[1])
    o = _attention("band", q, kv_all, g, wts["tab"], b=b, sq=nq, sk=sk, nh=nc, q0=q0, sk_valid=sk_valid)
    x_new = _mm([(o, wts["out"])], [F32], res=x)
    return x_new, kv32


def _even_weights(norm, w_in, gk, w_out, d):
    wa = d // 2
    n_idx = N_IDX_HEADS * IDX_DIM + IDX_DIM + N_IDX_HEADS
    n_idx_pad = -(-n_idx // LANE) * LANE
    o_idx = 4 * wa
    o_b = o_idx + n_idx
    w16 = _to_bf16(w_in)
    w16_out = _to_bf16(w_out)
    w16_b = w16[:, o_b:]
    return {
        "norm": norm, "gk": gk,
        "qa": _cols(w16, 0, wa), "kva": _cols(w16, wa, 2 * wa), "ga": _cols(w16, 3 * wa, wa),
        "idx": jnp.pad(w16[:, o_idx:o_b], ((0, 0), (0, n_idx_pad - n_idx))),
        "qb": _cols(w16_b, 0, wa), "kvb": _cols(w16_b, wa, 2 * wa), "gb": _cols(w16_b, 3 * wa, wa),
        "out_a": w16_out[:wa], "out_b": w16_out[wa:],
    }


def _odd_weights(norm, w_in, rel_bias, w_out, d):
    nc = rel_bias.shape[0]
    w16 = _to_bf16(w_in)
    rb = rel_bias * LOG2E
    lo, hi = rb[:, :1], rb[:, 2 * REL_CLIP:]
    by_dist = jnp.concatenate([jnp.broadcast_to(hi, (nc, 2 * TQ - 1 - REL_CLIP)), rb[:, ::-1],
                               jnp.broadcast_to(lo, (nc, TQ - 1 - REL_CLIP))], axis=1)
    window = lambda base: jnp.stack([by_dist[:, 2 * TQ - 1 - base - t:3 * TQ - 1 - base - t]
                                     for t in range(TQ)], axis=1)
    far = jnp.broadcast_to(hi[:, :, None], (nc, TQ, TQ))
    return {
        "norm": norm,
        "q": _cols(w16, 0, d), "kv": _cols(w16, d, 2 * d), "g": _cols(w16, 3 * d, d),
        "tab": jnp.stack([far, window(TQ), window(0)], axis=1),
        "out": _to_bf16(w_out),
    }


def kernel(x_prompt, x_sample, cache_a_kv, cache_a_kidx, cache_b_kv, cache_c_kv, norm_e, w_in_e,
           idx_k_gain, w_out_e, norm_o, w_in_o, rel_bias_o, w_out_o, norm_f):
    bp, sp, d = x_prompt.shape
    bs, ts, _ = x_sample.shape
    p_len = cache_a_kv.shape[2]
    depth = norm_e.shape[0] + norm_o.shape[0]
    assert sp % TQ == 0 and p_len % TQ == 0 and ts <= TQ
    assert REL_CLIP <= CHUNK and TQ == 2 * CHUNK

    xp = x_prompt.reshape(bp * sp, d)
    xs = x_sample.reshape(bs * ts, d)
    rows_p = lambda a, *tail: a.reshape(bp, sp, *tail)
    rows_s = lambda a, *tail: a.reshape(bs, ts, *tail)
    outs = {k: [] for k in ("a_kv_p", "a_ix_p", "b_kv_p", "c_kv_p", "a_kv_s", "a_ix_s", "b_kv_s", "c_kv_s")}
    for layer in range(depth):
        j = layer // 2
        if layer % 2 == 0:
            wts = _even_weights(norm_e[j], w_in_e[j], idx_k_gain[j], w_out_e[j], d)
            na = wts["out_a"].shape[0] // HEAD_DIM
            nb = wts["out_b"].shape[0] // HEAD_DIM
            xp, akv, aix, bkv = _even_layer(xp, wts, None, b=bp, nq=sp, q0=0)
            xs, akv2, aix2, bkv2 = _even_layer(xs, wts, (cache_a_kv[j], cache_a_kidx[j], cache_b_kv[j]),
                                               b=bs, nq=ts, q0=p_len)
            outs["a_kv_p"].append(rows_p(akv, 2, na, HEAD_DIM))
            outs["a_ix_p"].append(rows_p(aix, IDX_DIM))
            outs["b_kv_p"].append(rows_p(bkv, 2, nb, HEAD_DIM))
            outs["a_kv_s"].append(rows_s(akv2, 2, na, HEAD_DIM))
            outs["a_ix_s"].append(rows_s(aix2, IDX_DIM))
            outs["b_kv_s"].append(rows_s(bkv2, 2, nb, HEAD_DIM))
        else:
            wts = _odd_weights(norm_o[j], w_in_o[j], rel_bias_o[j], w_out_o[j], d)
            nc = wts["out"].shape[0] // HEAD_DIM
            wc = min(C_PAST_CHUNKS * CHUNK, sp)
            xp, ckv = _odd_layer(xp, wts, None, b=bp, nq=sp, q0=0)
            xs, ckv2 = _odd_layer(xs, wts, cache_c_kv[j], b=bs, nq=ts, q0=p_len)
            outs["c_kv_p"].append(rows_p(ckv, 2, nc, HEAD_DIM)[:, sp - wc:])
            outs["c_kv_s"].append(rows_s(ckv2, 2, nc, HEAD_DIM))
    y_prompt = _rmsnorm(xp, norm_f, F32).reshape(bp, sp, d)
    y_sample = _rmsnorm(xs, norm_f, F32).reshape(bs, ts, d)
    return (y_prompt, y_sample,
            jnp.stack(outs["a_kv_p"]), jnp.stack(outs["a_ix_p"]), jnp.stack(outs["b_kv_p"]), jnp.stack(outs["c_kv_p"]),
            jnp.stack(outs["a_kv_s"]), jnp.stack(outs["a_ix_s"]), jnp.stack(outs["b_kv_s"]), jnp.stack(outs["c_kv_s"]))
```

```python
import functools

import jax
import jax.numpy as jnp
from jax import lax
from jax.experimental import pallas as pl
from jax.experimental.pallas import tpu as pltpu

F32 = jnp.float32
BF16 = jnp.bfloat16

HEAD_DIM = 128
CHUNK = 64
CHUNK_SHIFT = 6
N_IDX_HEADS = 16
IDX_DIM = 64
TOPK_MAX = 256
C_PAST_CHUNKS = 8
REL_CLIP = 64
EPS = 1e-6
ATT_SCALE = HEAD_DIM ** -0.5
LOG2E = 1.4426950408889634
Q_SCALE = ATT_SCALE * LOG2E
IDX_SCALE = (IDX_DIM ** -0.5) * (N_IDX_HEADS ** -0.5)

TQ = 128
LANE = 128
BAND_BLOCKS = (C_PAST_CHUNKS * CHUNK) // TQ + 1
INT_MIN = -2 ** 31
NEG_INF = float("-inf")
VMEM_LIMIT = 56 * 1024 * 1024


def _cparams(sem):
    return pltpu.CompilerParams(dimension_semantics=sem, vmem_limit_bytes=VMEM_LIMIT)


def _pick(n, candidates):
    for c in candidates:
        if n % c == 0:
            return c
    raise ValueError(f"no block size in {candidates} divides {n}")


def _rmsnorm_kernel(x_ref, g_ref, o_ref):
    x = x_ref[...]
    ms = jnp.mean(x * x, axis=-1, keepdims=True)
    o_ref[...] = (x * lax.rsqrt(ms + EPS) * g_ref[...]).astype(o_ref.dtype)


def _rmsnorm(x, g, out_dtype):
    m, d = x.shape
    tm = _pick(m, (256, 128, 64, 32, 16, 8))
    return pl.pallas_call(
        _rmsnorm_kernel,
        grid=(m // tm,),
        in_specs=[pl.BlockSpec((tm, d), lambda i: (i, 0)),
                  pl.BlockSpec((1, d), lambda i: (0, 0))],
        out_specs=pl.BlockSpec((tm, d), lambda i: (i, 0)),
        out_shape=jax.ShapeDtypeStruct((m, d), out_dtype),
        compiler_params=_cparams(("parallel",)),
        name="rmsnorm",
    )(x, g.reshape(1, d))


def _iknorm_kernel(x_ref, g_ref, o_ref):
    x = x_ref[:, :IDX_DIM]
    ms = jnp.mean(x * x, axis=-1, keepdims=True)
    o_ref[...] = x * lax.rsqrt(ms + EPS) * g_ref[...]


def _iknorm(idx32, gk, col_block):
    m = idx32.shape[0]
    tm = _pick(m, (1024, 512, 256, 128, 64, 32, 16, 8))
    return pl.pallas_call(
        _iknorm_kernel,
        grid=(m // tm,),
        in_specs=[pl.BlockSpec((tm, LANE), lambda i: (i, col_block)),
                  pl.BlockSpec((1, IDX_DIM), lambda i: (0, 0))],
        out_specs=pl.BlockSpec((tm, IDX_DIM), lambda i: (i, 0)),
        out_shape=jax.ShapeDtypeStruct((m, IDX_DIM), F32),
        compiler_params=_cparams(("parallel",)),
        name="iknorm",
    )(idx32, gk.reshape(1, IDX_DIM))


def _mm_kernel(*refs, n_pairs, has_res, n_out, scale):
    acc = None
    for p in range(n_pairs):
        d = jnp.dot(refs[2 * p][...], refs[2 * p + 1][...], preferred_element_type=F32)
        acc = d if acc is None else acc + d
    pos = 2 * n_pairs
    if has_res:
        acc = acc + refs[pos][...]
        pos += 1
    if scale is not None:
        acc = acc * scale
    for o_ref in refs[pos:pos + n_out]:
        o_ref[...] = acc.astype(o_ref.dtype)


def _cols(w, lo, n):
    return (w, lo, n)


def _mm(pairs, out_dtypes, res=None, scale=None, tm=1024, tn=1024):
    pairs = [(a, w if isinstance(w, tuple) else (w, 0, w.shape[1])) for a, w in pairs]
    m = pairs[0][0].shape[0]
    n = pairs[0][1][2]
    tm = min(tm, m)
    tn = min(tn, n)
    assert m % tm == 0 and n % tn == 0, (m, n, tm, tn)
    in_specs, args = [], []
    for a, (w, lo, width) in pairs:
        k = a.shape[1]
        assert width == n and lo % tn == 0, (lo, width, tn)
        in_specs += [pl.BlockSpec((tm, k), lambda i, j: (i, 0)),
                     pl.BlockSpec((k, tn), lambda i, j, j0=lo // tn: (0, j0 + j))]
        args += [a, w]
    if res is not None:
        in_specs.append(pl.BlockSpec((tm, tn), lambda i, j: (i, j)))
        args.append(res)
    outs = pl.pallas_call(
        functools.partial(_mm_kernel, n_pairs=len(pairs), has_res=res is not None,
                          n_out=len(out_dtypes), scale=scale),
        grid=(m // tm, n // tn),
        in_specs=in_specs,
        out_specs=[pl.BlockSpec((tm, tn), lambda i, j: (i, j)) for _ in out_dtypes],
        out_shape=[jax.ShapeDtypeStruct((m, n), dt) for dt in out_dtypes],
        compiler_params=_cparams(("parallel", "arbitrary")),
        name="matmul",
    )(*args)
    return outs[0] if len(out_dtypes) == 1 else tuple(outs)


def _nt_dot(a, b):
    return lax.dot_general(a, b, (((1,), (1,)), ((), ())), preferred_element_type=F32)


def _dsa_mask_kernel(iq_ref, ik_ref, iwt_ref, o_ref, keys_ref, *, q0, kb, nkbt, sk_valid, topk):
    i = pl.program_id(1)
    nkb = jnp.minimum(nkbt, (q0 + (i + 1) * TQ + kb - 1) // kb)
    w = iwt_ref[...] * IDX_SCALE
    qchunk = (q0 + i * TQ + lax.broadcasted_iota(jnp.int32, (1, TQ), 1)) >> CHUNK_SHIFT

    def build(t, _):
        off = pl.multiple_of(t * kb, kb)
        acc = jnp.zeros((kb, TQ), F32)
        for h in range(N_IDX_HEADS):
            e, p = h % 2, h // 2
            ikh = ik_ref[pl.ds(off, kb), e * LANE:(e + 1) * LANE]
            s = _nt_dot(ikh, iq_ref[:, p * LANE:(p + 1) * LANE])
            acc = acc + jnp.maximum(s, 0.0) * w[h:h + 1, :]
        acc = acc + 0.0
        bits = lax.bitcast_convert_type(acc, jnp.int32)
        key = jnp.where(bits < 0, bits ^ 0x7FFFFFFF, bits)
        kpos = off + lax.broadcasted_iota(jnp.int32, (kb, TQ), 0)
        adm = ((kpos >> CHUNK_SHIFT) <= qchunk) & (kpos < sk_valid)
        keys_ref[pl.ds(off, kb), :] = jnp.where(adm, key, INT_MIN)
        return 0

    lax.fori_loop(0, nkb, build, 0)

    def count(pred):
        def body(t, acc8):
            off = pl.multiple_of(t * kb, kb)
            hit = pred(keys_ref[pl.ds(off, kb), :]).astype(jnp.int32)
            return acc8 + hit.reshape(kb // 8, 8, TQ).sum(axis=0)
        acc8 = lax.fori_loop(0, nkb, body, jnp.zeros((8, TQ), jnp.int32))
        return acc8.sum(axis=0, keepdims=True)

    thr = jnp.where(count(lambda kx: kx >= 0) >= topk, 0, INT_MIN).astype(jnp.int32)

    def bit_step(t, thr):
        cand = thr | (jnp.int32(1) << (30 - t))
        return jnp.where(count(lambda kx: kx >= cand) >= topk, cand, thr)

    thr = lax.fori_loop(0, 31, bit_step, thr)
    n_ge = count(lambda kx: kx >= thr)
    tie = jnp.max(jnp.where(thr == INT_MIN, 0, n_ge)) > topk

    @pl.when(jnp.logical_not(tie))
    def _():
        def emit(t, _):
            off = pl.multiple_of(t * kb, kb)
            kx = keys_ref[pl.ds(off, kb), :]
            sel = (kx >= thr) & (kx != INT_MIN)
            o_ref[0, 0, t] = jnp.where(sel, 0.0, NEG_INF).T.astype(o_ref.dtype)
            return 0

        lax.fori_loop(0, nkb, emit, 0)

    @pl.when(tie)
    def _():
        need = (topk - count(lambda kx: kx > thr)).astype(F32)
        tri = (lax.broadcasted_iota(jnp.int32, (kb, kb), 0)
               >= lax.broadcasted_iota(jnp.int32, (kb, kb), 1)).astype(BF16)

        def emit(t, seen):
            off = pl.multiple_of(t * kb, kb)
            kx = keys_ref[pl.ds(off, kb), :]
            eq = kx == thr
            rank = jnp.dot(tri, eq.astype(BF16), preferred_element_type=F32) + seen
            sel = ((kx > thr) | (eq & (rank <= need))) & (kx != INT_MIN)
            o_ref[0, 0, t] = jnp.where(sel, 0.0, NEG_INF).T.astype(o_ref.dtype)
            return rank[kb - 1:kb, :]

        lax.fori_loop(0, nkb, emit, jnp.zeros((1, TQ), F32))

    def fill(t, _):
        o_ref[0, 0, t] = jnp.full((TQ, kb), NEG_INF, o_ref.dtype)
        return 0

    lax.fori_loop(nkb, nkbt, fill, 0)


def _dsa_mask(idx16, ik2, iwt, *, b, sq, sk, q0, sk_valid, topk, kb):
    nqt, nkbt = sq // TQ, sk // kb
    n_iq = N_IDX_HEADS * IDX_DIM
    return pl.pallas_call(
        functools.partial(_dsa_mask_kernel, q0=q0, kb=kb, nkbt=nkbt, sk_valid=sk_valid, topk=topk),
        grid=(b, nqt),
        in_specs=[pl.BlockSpec((TQ, n_iq), lambda bi, i: (bi * nqt + i, 0)),
                  pl.BlockSpec((sk, 2 * LANE), lambda bi, i: (bi, 0)),
                  pl.BlockSpec((N_IDX_HEADS, TQ), lambda bi, i: (0, bi * nqt + i))],
        out_specs=pl.BlockSpec((1, 1, nkbt, TQ, kb), lambda bi, i: (bi, i, 0, 0, 0)),
        out_shape=jax.ShapeDtypeStruct((b, nqt, nkbt, TQ, kb), BF16),
        scratch_shapes=[pltpu.VMEM((sk, TQ), jnp.int32)],
        compiler_params=_cparams(("parallel", "arbitrary")),
        name="dsa_mask",
    )(idx16, ik2, iwt)


def _gated(o, g):
    return o * (g * jax.nn.sigmoid(g))


def _head(g):
    return slice(g * HEAD_DIM, (g + 1) * HEAD_DIM)


def _dsa_attn_kernel(q_ref, k_ref, v_ref, m_ref, sl_ref, g_ref, o_ref, *, q0, kb, nkbt, gh):
    i = pl.program_id(2)
    tq = q_ref.shape[0]
    nkb = jnp.minimum(nkbt, (q0 + (i + 1) * tq + kb - 1) // kb)
    qpos = q0 + i * tq + lax.broadcasted_iota(jnp.int32, (tq, kb), 0)
    col = lax.broadcasted_iota(jnp.int32, (tq, kb), 1)

    def body(t, carry):
        off = pl.multiple_of(t * kb, kb)
        dist = jnp.abs(qpos - (off + col)).astype(F32)
        if tq > TQ:
            mask = jnp.concatenate([m_ref[0, u, t] for u in range(tq // TQ)], axis=0).astype(F32)
        else:
            mask = m_ref[0, 0, t, :tq, :].astype(F32)
        qk = [_nt_dot(q_ref[:, _head(g)], k_ref[pl.ds(off, kb), _head(g)]) for g in range(gh)]
        soft = []
        for g in range(gh):
            m, l, _ = carry[g]
            s = qk[g] - (sl_ref[g][:, :1] * LOG2E) * dist + mask
            m_new = jnp.maximum(m, s.max(axis=1, keepdims=True))
            m_use = jnp.where(m_new == NEG_INF, 0.0, m_new)
            alpha = jnp.exp2(m - m_use)
            p = jnp.exp2(s - m_use)
            soft.append((m_new, alpha * l + p.sum(axis=1, keepdims=True), alpha, p.astype(BF16)))
        pv = [jnp.dot(soft[g][3], v_ref[pl.ds(off, kb), _head(g)], preferred_element_type=F32)
              for g in range(gh)]
        return tuple((soft[g][0], soft[g][1], soft[g][2] * carry[g][2] + pv[g]) for g in range(gh))

    init = tuple((jnp.full((tq, 1), NEG_INF, F32), jnp.zeros((tq, 1), F32),
                  jnp.zeros((tq, HEAD_DIM), F32)) for _ in range(gh))
    res = lax.fori_loop(0, nkb, body, init)
    for g in range(gh):
        _, l, acc = res[g]
        o_ref[:, _head(g)] = _gated(acc / l, g_ref[:, _head(g)]).astype(o_ref.dtype)


def _stick_kernel(q_ref, k_ref, v_ref, g_ref, o_ref, *, q0, kb, nkbt, gh):
    i = pl.program_id(2)
    tq = q_ref.shape[0]
    nkb = jnp.minimum(nkbt, (q0 + (i + 1) * tq + kb - 1) // kb)
    from_here = (lax.broadcasted_iota(jnp.int32, (kb, kb), 0)
                 >= lax.broadcasted_iota(jnp.int32, (kb, kb), 1)).astype(BF16)
    qpos = q0 + i * tq + lax.broadcasted_iota(jnp.int32, (tq, kb), 0)
    col = lax.broadcasted_iota(jnp.int32, (tq, kb), 1)

    def block(off, carry, diagonal):
        causal = (off + col) < qpos
        qk = [_nt_dot(q_ref[:, _head(g)], k_ref[pl.ds(off, kb), _head(g)]) for g in range(gh)]
        drops = []
        for g in range(gh):
            z2 = qk[g]
            neg_abs = lax.bitcast_convert_type(lax.bitcast_convert_type(z2, jnp.int32) | INT_MIN, F32)
            drop = jnp.maximum(z2, 0.0) + jnp.log2(1.0 + jnp.exp2(neg_abs))
            if diagonal:
                drop = jnp.where(causal, drop, 0.0)
            drops.append(drop)
        incl = []
        for g in range(gh):
            hi = drops[g].astype(BF16)
            lo = (drops[g] - hi.astype(F32)).astype(BF16)
            incl.append(jnp.dot(hi, from_here, preferred_element_type=F32)
                        + jnp.dot(lo, from_here, preferred_element_type=F32))
        w = []
        for g in range(gh):
            wg = jnp.exp2(qk[g] - carry[g][0] - incl[g])
            if diagonal:
                wg = jnp.where(causal, wg, 0.0)
            w.append(wg.astype(BF16))
        pv = [jnp.dot(w[g], v_ref[pl.ds(off, kb), _head(g)], preferred_element_type=F32) for g in range(gh)]
        return tuple((carry[g][0] + incl[g][:, :1], carry[g][1] + pv[g]) for g in range(gh))

    init = tuple((jnp.zeros((tq, 1), F32), jnp.zeros((tq, HEAD_DIM), F32)) for _ in range(gh))
    first = block(pl.multiple_of((nkb - 1) * kb, kb), init, True)
    res = lax.fori_loop(1, nkb, lambda t, c: block(pl.multiple_of((nkb - 1 - t) * kb, kb), c, False), first)
    for g in range(gh):
        o_ref[:, _head(g)] = _gated(res[g][1], g_ref[:, _head(g)]).astype(o_ref.dtype)


def _band_kernel(q_ref, k_ref, v_ref, tab_ref, g_ref, o_ref, *, q0, sk_valid, gh):
    i = pl.program_id(2)
    tq = q_ref.shape[0]
    jabs = (q0 + i * tq) // TQ
    back = BAND_BLOCKS - 1
    shift = back - jnp.minimum(jabs, back)
    s0 = pl.multiple_of(jnp.maximum(jabs - back, 0) * TQ, TQ)
    width = BAND_BLOCKS * TQ
    qpos = q0 + i * tq + lax.broadcasted_iota(jnp.int32, (tq, width), 0)
    kpos = s0 + lax.broadcasted_iota(jnp.int32, (tq, width), 1)
    qc, kc = qpos >> CHUNK_SHIFT, kpos >> CHUNK_SHIFT
    allowed = (kc <= qc) & (kc >= qc - C_PAST_CHUNKS) & (kpos < sk_valid)
    qk = [_nt_dot(q_ref[:, _head(g)], k_ref[pl.ds(s0, width), _head(g)]) for g in range(gh)]
    entry = [jnp.clip(r + shift - (back - 2), 0, 2) for r in range(BAND_BLOCKS)]
    soft = []
    for g in range(gh):
        bias = jnp.concatenate([tab_ref[g, entry[r], :tq, :] for r in range(BAND_BLOCKS)], axis=1)
        s = jnp.where(allowed, qk[g] + bias, NEG_INF)
        p = jnp.exp2(s - s.max(axis=1, keepdims=True))
        soft.append((p.sum(axis=1, keepdims=True), p.astype(BF16)))
    pv = [jnp.dot(soft[g][1], v_ref[pl.ds(s0, width), _head(g)], preferred_element_type=F32)
          for g in range(gh)]
    for g in range(gh):
        o_ref[:, _head(g)] = _gated(pv[g] / soft[g][0], g_ref[:, _head(g)]).astype(o_ref.dtype)


def _attention(kind, q, kv, gate, extra, *, b, sq, sk, nh, q0, sk_valid=None, kb=None, gh=8):
    tq = min(TQ if kind == "band" else 2 * TQ, sq)
    nqt = sq // tq
    assert nh % gh == 0 and sq % tq == 0 and q0 % TQ == 0 and tq % 16 == 0
    ng = nh // gh
    gw = gh * HEAD_DIM
    q_spec = pl.BlockSpec((tq, gw), lambda bi, h, i: (bi * nqt + i, h))
    k_spec = pl.BlockSpec((sk, gw), lambda bi, h, i: (bi, h))
    v_spec = pl.BlockSpec((sk, gw), lambda bi, h, i: (bi, ng + h))
    head_vec = pl.BlockSpec((gh, 1, LANE), lambda bi, h, i: (h, 0, 0))
    if kind == "dsa":
        mask, slopes = extra
        nkbt = sk // kb
        body = functools.partial(_dsa_attn_kernel, q0=q0, kb=kb, nkbt=nkbt, gh=gh)
        extra_specs = [pl.BlockSpec((1, max(tq // TQ, 1), nkbt, TQ, kb), lambda bi, h, i: (bi, i, 0, 0, 0)),
                       head_vec]
        extra_args = [mask, slopes]
    elif kind == "stick":
        assert kb % tq == 0 and q0 % tq == 0
        body = functools.partial(_stick_kernel, q0=q0, kb=kb, nkbt=sk // kb, gh=gh)
        extra_specs, extra_args = [], []
    else:
        body = functools.partial(_band_kernel, q0=q0, sk_valid=sk_valid, gh=gh)
        extra_specs = [pl.BlockSpec((gh, 3, TQ, TQ), lambda bi, h, i: (h, 0, 0, 0))]
        extra_args = [extra]
    return pl.pallas_call(
        body,
        grid=(b, ng, nqt),
        in_specs=[q_spec, k_spec, v_spec] + extra_specs + [q_spec],
        out_specs=q_spec,
        out_shape=jax.ShapeDtypeStruct((b * sq, nh * HEAD_DIM), BF16),
        compiler_params=_cparams(("parallel", "parallel", "arbitrary")),
        name=kind + "_attention",
    )(q, kv, kv, *extra_args, gate)


def _pad_q(a, b, nq, sq):
    if nq == sq:
        return a
    return jnp.pad(a.reshape(b, nq, -1), ((0, 0), (0, sq - nq), (0, 0))).reshape(b * sq, -1)


def _with_cache(cache, new, b, nq, sq, lead):
    width = new.shape[-1]
    parts = [jnp.zeros((b, lead, width), BF16)] if lead else []
    parts += [cache.reshape(b, cache.shape[1], width).astype(BF16), new.reshape(b, nq, width).astype(BF16),
              jnp.zeros((b, sq - nq, width), BF16)]
    return jnp.concatenate(parts, axis=1).reshape(-1, width)


def _kv_rows_kernel(c_ref, n_ref, o_ref, *, lead_blocks, cache_blocks, nq, nsub):
    r = pl.program_id(1)
    rows, width = o_ref.shape[1], o_ref.shape[2]

    @pl.when(r < lead_blocks)
    def _():
        o_ref[...] = jnp.zeros(o_ref.shape, o_ref.dtype)

    @pl.when((r >= lead_blocks) & (r < lead_blocks + cache_blocks))
    def _():
        by_sub = jnp.swapaxes(c_ref[...], 0, 1)
        for c in range(nsub):
            o_ref[0, :, c * LANE:(c + 1) * LANE] = by_sub[c].astype(o_ref.dtype)

    @pl.when(r == lead_blocks + cache_blocks)
    def _():
        o_ref[0, :nq, :] = n_ref[0]
        o_ref[0, nq:, :] = jnp.zeros((rows - nq, width), o_ref.dtype)


def _kv_rows(cache, new, *, b, nq, sq, lead):
    width = new.shape[-1]
    p = cache.shape[1]
    nsub = width // LANE
    assert sq == TQ and p % TQ == 0 and lead % TQ == 0 and nq % 16 == 0
    lead_blocks, cache_blocks = lead // TQ, p // TQ
    cache_block = lambda bi, r: (bi * cache_blocks + jnp.clip(r - lead_blocks, 0, cache_blocks - 1), 0, 0)
    return pl.pallas_call(
        functools.partial(_kv_rows_kernel, lead_blocks=lead_blocks, cache_blocks=cache_blocks, nq=nq, nsub=nsub),
        grid=(b, lead_blocks + cache_blocks + 1),
        in_specs=[pl.BlockSpec((TQ, nsub, LANE), cache_block),
                  pl.BlockSpec((1, nq, width), lambda bi, r: (bi, 0, 0))],
        out_specs=pl.BlockSpec((1, TQ, width), lambda bi, r: (bi, r, 0)),
        out_shape=jax.ShapeDtypeStruct((b, lead + p + sq, width), BF16),
        compiler_params=_cparams(("parallel", "arbitrary")),
        name="kv_rows",
    )(cache.reshape(b * p, nsub, LANE), new.reshape(b, nq, width)).reshape(-1, width)


def _even_layer(x, wts, cache, *, b, nq, q0):
    sq = -(-nq // TQ) * TQ
    na = wts["out_a"].shape[0] // HEAD_DIM
    nb = wts["out_b"].shape[0] // HEAD_DIM
    h = _rmsnorm(x, wts["norm"], BF16)
    qa = _mm([(h, wts["qa"])], [BF16], scale=Q_SCALE)
    kva32, kva16 = _mm([(h, wts["kva"])], [F32, BF16])
    ga = _mm([(h, wts["ga"])], [F32])
    idx32, idx16 = _mm([(h, wts["idx"])], [F32, BF16], tm=512, tn=wts["idx"].shape[1])
    qb = _mm([(h, wts["qb"])], [BF16], scale=Q_SCALE)
    kvb32, kvb16 = _mm([(h, wts["kvb"])], [F32, BF16])
    gb = _mm([(h, wts["gb"])], [F32])

    n_iq = N_IDX_HEADS * IDX_DIM
    ikn = _iknorm(idx32, wts["gk"], n_iq // LANE)
    iw = idx32[:, n_iq + IDX_DIM:n_iq + IDX_DIM + N_IDX_HEADS]
    iwt = _pad_q(iw, b, nq, sq).T

    if cache is None:
        sk = sk_valid = sq
        ka_all, kb_all = kva16, kvb16
        ik_all = ikn.reshape(b, sk, IDX_DIM).astype(BF16)
    else:
        a_kv, a_kidx, b_kv = cache
        sk, sk_valid = q0 + sq, q0 + nq
        ka_all = _kv_rows(a_kv, kva16, b=b, nq=nq, sq=sq, lead=0)
        kb_all = _kv_rows(b_kv, kvb16, b=b, nq=nq, sq=sq, lead=0)
        ik_all = _with_cache(a_kidx, ikn, b, nq, sq, 0).reshape(b, sk, IDX_DIM)
    zeros = jnp.zeros_like(ik_all)
    ik2 = jnp.concatenate([ik_all, zeros, zeros, ik_all], axis=-1).reshape(b * sk, 2 * LANE)

    pad = functools.partial(_pad_q, b=b, nq=nq, sq=sq)
    kb_d = _pick(sk, (512, 384, 256, 128))
    mask = _dsa_mask(pad(idx16), ik2, iwt, b=b, sq=sq, sk=sk, q0=q0, sk_valid=sk_valid,
                     topk=min(TOPK_MAX, sk_valid // 4), kb=kb_d)
    slopes = 2.0 ** (-8.0 * jnp.arange(1, na + 1, dtype=F32) / na)
    slopes = jnp.broadcast_to(slopes[:, None, None], (na, 1, LANE))
    oa = _attention("dsa", qa, ka_all, ga, (mask, slopes), b=b, sq=nq, sk=sk, nh=na, q0=q0, kb=kb_d, gh=4)
    ob = _attention("stick", qb, kb_all, gb, None, b=b, sq=nq, sk=sk, nh=nb, q0=q0,
                    kb=_pick(sk, (256, 384, 128)))
    x_new = _mm([(oa, wts["out_a"]), (ob, wts["out_b"])], [F32], res=x)
    return x_new, kva32, ikn, kvb32


def _odd_layer(x, wts, cache, *, b, nq, q0):
    sq = -(-nq // TQ) * TQ
    nc = wts["out"].shape[0] // HEAD_DIM
    h = _rmsnorm(x, wts["norm"], BF16)
    q = _mm([(h, wts["q"])], [BF16], scale=Q_SCALE)
    kv32, kv16 = _mm([(h, wts["kv"])], [F32, BF16])
    g = _mm([(h, wts["g"])], [F32])
    if cache is None:
        sk = sk_valid = sq
        kv_all = kv16
    else:
        sk, sk_valid = q0 + sq, q0 + nq
        kv_all = _kv_rows(cache, kv16, b=b, nq=nq, sq=sq, lead=q0 - cache.shape[1])
    o = _attention("band", q, kv_all, g, wts["tab"], b=b, sq=nq, sk=sk, nh=nc, q0=q0, sk_valid=sk_valid)
    x_new = _mm([(o, wts["out"])], [F32], res=x)
    return x_new, kv32


def _even_weights(norm, w_in, gk, w_out, d):
    wa = d // 2
    n_idx = N_IDX_HEADS * IDX_DIM + IDX_DIM + N_IDX_HEADS
    n_idx_pad = -(-n_idx // LANE) * LANE
    o_idx = 4 * wa
    o_b = o_idx + n_idx
    w16 = w_in.astype(BF16)
    w16_out = w_out.astype(BF16)
    w16_b = w16[:, o_b:]
    return {
        "norm": norm, "gk": gk,
        "qa": _cols(w16, 0, wa), "kva": _cols(w16, wa, 2 * wa), "ga": _cols(w16, 3 * wa, wa),
        "idx": jnp.pad(w16[:, o_idx:o_b], ((0, 0), (0, n_idx_pad - n_idx))),
        "qb": _cols(w16_b, 0, wa), "kvb": _cols(w16_b, wa, 2 * wa), "gb": _cols(w16_b, 3 * wa, wa),
        "out_a": w16_out[:wa], "out_b": w16_out[wa:],
    }


def _odd_weights(norm, w_in, rel_bias, w_out, d):
    nc = rel_bias.shape[0]
    w16 = w_in.astype(BF16)
    rb = rel_bias * LOG2E
    lo, hi = rb[:, :1], rb[:, 2 * REL_CLIP:]
    by_dist = jnp.concatenate([jnp.broadcast_to(hi, (nc, 2 * TQ - 1 - REL_CLIP)), rb[:, ::-1],
                               jnp.broadcast_to(lo, (nc, TQ - 1 - REL_CLIP))], axis=1)
    window = lambda base: jnp.stack([by_dist[:, 2 * TQ - 1 - base - t:3 * TQ - 1 - base - t]
                                     for t in range(TQ)], axis=1)
    far = jnp.broadcast_to(hi[:, :, None], (nc, TQ, TQ))
    return {
        "norm": norm,
        "q": _cols(w16, 0, d), "kv": _cols(w16, d, 2 * d), "g": _cols(w16, 3 * d, d),
        "tab": jnp.stack([far, window(TQ), window(0)], axis=1),
        "out": w_out.astype(BF16),
    }


def kernel(x_prompt, x_sample, cache_a_kv, cache_a_kidx, cache_b_kv, cache_c_kv, norm_e, w_in_e,
           idx_k_gain, w_out_e, norm_o, w_in_o, rel_bias_o, w_out_o, norm_f):
    bp, sp, d = x_prompt.shape
    bs, ts, _ = x_sample.shape
    p_len = cache_a_kv.shape[2]
    depth = norm_e.shape[0] + norm_o.shape[0]
    assert sp % TQ == 0 and p_len % TQ == 0 and ts <= TQ
    assert REL_CLIP <= CHUNK and TQ == 2 * CHUNK

    xp = x_prompt.reshape(bp * sp, d)
    xs = x_sample.reshape(bs * ts, d)
    rows_p = lambda a, *tail: a.reshape(bp, sp, *tail)
    rows_s = lambda a, *tail: a.reshape(bs, ts, *tail)
    outs = {k: [] for k in ("a_kv_p", "a_ix_p", "b_kv_p", "c_kv_p", "a_kv_s", "a_ix_s", "b_kv_s", "c_kv_s")}
    for layer in range(depth):
        j = layer // 2
        if layer % 2 == 0:
            wts = _even_weights(norm_e[j], w_in_e[j], idx_k_gain[j], w_out_e[j], d)
            na = wts["out_a"].shape[0] // HEAD_DIM
            nb = wts["out_b"].shape[0] // HEAD_DIM
            xp, akv, aix, bkv = _even_layer(xp, wts, None, b=bp, nq=sp, q0=0)
            xs, akv2, aix2, bkv2 = _even_layer(xs, wts, (cache_a_kv[j], cache_a_kidx[j], cache_b_kv[j]),
                                               b=bs, nq=ts, q0=p_len)
            outs["a_kv_p"].append(rows_p(akv, 2, na, HEAD_DIM))
            outs["a_ix_p"].append(rows_p(aix, IDX_DIM))
            outs["b_kv_p"].append(rows_p(bkv, 2, nb, HEAD_DIM))
            outs["a_kv_s"].append(rows_s(akv2, 2, na, HEAD_DIM))
            outs["a_ix_s"].append(rows_s(aix2, IDX_DIM))
            outs["b_kv_s"].append(rows_s(bkv2, 2, nb, HEAD_DIM))
        else:
            wts = _odd_weights(norm_o[j], w_in_o[j], rel_bias_o[j], w_out_o[j], d)
            nc = wts["out"].shape[0] // HEAD_DIM
            wc = min(C_PAST_CHUNKS * CHUNK, sp)
            xp, ckv = _odd_layer(xp, wts, None, b=bp, nq=sp, q0=0)
            xs, ckv2 = _odd_layer(xs, wts, cache_c_kv[j], b=bs, nq=ts, q0=p_len)
            outs["c_kv_p"].append(rows_p(ckv, 2, nc, HEAD_DIM)[:, sp - wc:])
            outs["c_kv_s"].append(rows_s(ckv2, 2, nc, HEAD_DIM))
    y_prompt = _rmsnorm(xp, norm_f, F32).reshape(bp, sp, d)
    y_sample = _rmsnorm(xs, norm_f, F32).reshape(bs, ts, d)
    return (y_prompt, y_sample,
            jnp.stack(outs["a_kv_p"]), jnp.stack(outs["a_ix_p"]), jnp.stack(outs["b_kv_p"]), jnp.stack(outs["c_kv_p"]),
            jnp.stack(outs["a_kv_s"]), jnp.stack(outs["a_ix_s"]), jnp.stack(outs["b_kv_s"]), jnp.stack(outs["c_kv_s"]))
```

```python
import functools

import jax
import jax.numpy as jnp
from jax import lax
from jax.experimental import pallas as pl
from jax.experimental.pallas import tpu as pltpu

F32 = jnp.float32
BF16 = jnp.bfloat16

HEAD_DIM = 128
CHUNK = 64
CHUNK_SHIFT = 6
N_IDX_HEADS = 16
IDX_DIM = 64
TOPK_MAX = 256
C_PAST_CHUNKS = 8
REL_CLIP = 64
EPS = 1e-6
ATT_SCALE = HEAD_DIM ** -0.5
LOG2E = 1.4426950408889634
Q_SCALE = ATT_SCALE * LOG2E
IDX_SCALE = (IDX_DIM ** -0.5) * (N_IDX_HEADS ** -0.5)

TQ = 128
LANE = 128
BAND_BLOCKS = (C_PAST_CHUNKS * CHUNK) // TQ + 1
INT_MIN = -2 ** 31
NEG_INF = float("-inf")
VMEM_LIMIT = 56 * 1024 * 1024


def _cparams(sem):
    return pltpu.CompilerParams(dimension_semantics=sem, vmem_limit_bytes=VMEM_LIMIT)


def _pick(n, candidates):
    for c in candidates:
        if n % c == 0:
            return c
    raise ValueError(f"no block size in {candidates} divides {n}")


def _rmsnorm_kernel(x_ref, g_ref, o_ref):
    x = x_ref[...]
    ms = jnp.mean(x * x, axis=-1, keepdims=True)
    o_ref[...] = (x * lax.rsqrt(ms + EPS) * g_ref[...]).astype(o_ref.dtype)


def _rmsnorm(x, g, out_dtype):
    m, d = x.shape
    tm = _pick(m, (256, 128, 64, 32, 16, 8))
    return pl.pallas_call(
        _rmsnorm_kernel,
        grid=(m // tm,),
        in_specs=[pl.BlockSpec((tm, d), lambda i: (i, 0)),
                  pl.BlockSpec((1, d), lambda i: (0, 0))],
        out_specs=pl.BlockSpec((tm, d), lambda i: (i, 0)),
        out_shape=jax.ShapeDtypeStruct((m, d), out_dtype),
        compiler_params=_cparams(("parallel",)),
        name="rmsnorm",
    )(x, g.reshape(1, d))


def _iknorm_kernel(x_ref, g_ref, o_ref):
    x = x_ref[:, :IDX_DIM]
    ms = jnp.mean(x * x, axis=-1, keepdims=True)
    o_ref[...] = x * lax.rsqrt(ms + EPS) * g_ref[...]


def _iknorm(idx32, gk, col_block):
    m = idx32.shape[0]
    tm = _pick(m, (1024, 512, 256, 128, 64, 32, 16, 8))
    return pl.pallas_call(
        _iknorm_kernel,
        grid=(m // tm,),
        in_specs=[pl.BlockSpec((tm, LANE), lambda i: (i, col_block)),
                  pl.BlockSpec((1, IDX_DIM), lambda i: (0, 0))],
        out_specs=pl.BlockSpec((tm, IDX_DIM), lambda i: (i, 0)),
        out_shape=jax.ShapeDtypeStruct((m, IDX_DIM), F32),
        compiler_params=_cparams(("parallel",)),
        name="iknorm",
    )(idx32, gk.reshape(1, IDX_DIM))


def _mm_kernel(*refs, n_pairs, has_res, n_out, scale):
    acc = None
    for p in range(n_pairs):
        d = jnp.dot(refs[2 * p][...], refs[2 * p + 1][...], preferred_element_type=F32)
        acc = d if acc is None else acc + d
    pos = 2 * n_pairs
    if has_res:
        acc = acc + refs[pos][...]
        pos += 1
    if scale is not None:
        acc = acc * scale
    for o_ref in refs[pos:pos + n_out]:
        o_ref[...] = acc.astype(o_ref.dtype)


def _cols(w, lo, n):
    return (w, lo, n)


def _mm(pairs, out_dtypes, res=None, scale=None, tm=1024, tn=1024):
    pairs = [(a, w if isinstance(w, tuple) else (w, 0, w.shape[1])) for a, w in pairs]
    m = pairs[0][0].shape[0]
    n = pairs[0][1][2]
    tm = min(tm, m)
    tn = min(tn, n)
    assert m % tm == 0 and n % tn == 0, (m, n, tm, tn)
    in_specs, args = [], []
    for a, (w, lo, width) in pairs:
        k = a.shape[1]
        assert width == n and lo % tn == 0, (lo, width, tn)
        in_specs += [pl.BlockSpec((tm, k), lambda i, j: (i, 0)),
                     pl.BlockSpec((k, tn), lambda i, j, j0=lo // tn: (0, j0 + j))]
        args += [a, w]
    if res is not None:
        in_specs.append(pl.BlockSpec((tm, tn), lambda i, j: (i, j)))
        args.append(res)
    outs = pl.pallas_call(
        functools.partial(_mm_kernel, n_pairs=len(pairs), has_res=res is not None,
                          n_out=len(out_dtypes), scale=scale),
        grid=(m // tm, n // tn),
        in_specs=in_specs,
        out_specs=[pl.BlockSpec((tm, tn), lambda i, j: (i, j)) for _ in out_dtypes],
        out_shape=[jax.ShapeDtypeStruct((m, n), dt) for dt in out_dtypes],
        compiler_params=_cparams(("parallel", "arbitrary")),
        name="matmul",
    )(*args)
    return outs[0] if len(out_dtypes) == 1 else tuple(outs)


def _nt_dot(a, b):
    return lax.dot_general(a, b, (((1,), (1,)), ((), ())), preferred_element_type=F32)


def _dsa_mask_kernel(iq_ref, ik_ref, iwt_ref, o_ref, keys_ref, *, q0, kb, nkbt, sk_valid, topk):
    i = pl.program_id(1)
    nkb = jnp.minimum(nkbt, (q0 + (i + 1) * TQ + kb - 1) // kb)
    w = iwt_ref[...] * IDX_SCALE
    qchunk = (q0 + i * TQ + lax.broadcasted_iota(jnp.int32, (1, TQ), 1)) >> CHUNK_SHIFT

    def build(t, _):
        off = pl.multiple_of(t * kb, kb)
        acc = jnp.zeros((kb, TQ), F32)
        for h in range(N_IDX_HEADS):
            e, p = h % 2, h // 2
            ikh = ik_ref[pl.ds(off, kb), e * LANE:(e + 1) * LANE]
            s = _nt_dot(ikh, iq_ref[:, p * LANE:(p + 1) * LANE])
            acc = acc + jnp.maximum(s, 0.0) * w[h:h + 1, :]
        acc = acc + 0.0
        bits = lax.bitcast_convert_type(acc, jnp.int32)
        key = jnp.where(bits < 0, bits ^ 0x7FFFFFFF, bits)
        kpos = off + lax.broadcasted_iota(jnp.int32, (kb, TQ), 0)
        adm = ((kpos >> CHUNK_SHIFT) <= qchunk) & (kpos < sk_valid)
        keys_ref[pl.ds(off, kb), :] = jnp.where(adm, key, INT_MIN)
        return 0

    lax.fori_loop(0, nkb, build, 0)

    def count(pred):
        def body(t, acc8):
            off = pl.multiple_of(t * kb, kb)
            hit = pred(keys_ref[pl.ds(off, kb), :]).astype(jnp.int32)
            return acc8 + hit.reshape(kb // 8, 8, TQ).sum(axis=0)
        acc8 = lax.fori_loop(0, nkb, body, jnp.zeros((8, TQ), jnp.int32))
        return acc8.sum(axis=0, keepdims=True)

    thr = jnp.where(count(lambda kx: kx >= 0) >= topk, 0, INT_MIN).astype(jnp.int32)

    def bit_step(t, thr):
        cand = thr | (jnp.int32(1) << (30 - t))
        return jnp.where(count(lambda kx: kx >= cand) >= topk, cand, thr)

    thr = lax.fori_loop(0, 31, bit_step, thr)
    n_ge = count(lambda kx: kx >= thr)
    tie = jnp.max(jnp.where(thr == INT_MIN, 0, n_ge)) > topk

    @pl.when(jnp.logical_not(tie))
    def _():
        def emit(t, _):
            off = pl.multiple_of(t * kb, kb)
            kx = keys_ref[pl.ds(off, kb), :]
            sel = (kx >= thr) & (kx != INT_MIN)
            o_ref[0, 0, t] = jnp.where(sel, 0.0, NEG_INF).T.astype(o_ref.dtype)
            return 0

        lax.fori_loop(0, nkb, emit, 0)

    @pl.when(tie)
    def _():
        need = (topk - count(lambda kx: kx > thr)).astype(F32)
        tri = (lax.broadcasted_iota(jnp.int32, (kb, kb), 0)
               >= lax.broadcasted_iota(jnp.int32, (kb, kb), 1)).astype(BF16)

        def emit(t, seen):
            off = pl.multiple_of(t * kb, kb)
            kx = keys_ref[pl.ds(off, kb), :]
            eq = kx == thr
            rank = jnp.dot(tri, eq.astype(BF16), preferred_element_type=F32) + seen
            sel = ((kx > thr) | (eq & (rank <= need))) & (kx != INT_MIN)
            o_ref[0, 0, t] = jnp.where(sel, 0.0, NEG_INF).T.astype(o_ref.dtype)
            return rank[kb - 1:kb, :]

        lax.fori_loop(0, nkb, emit, jnp.zeros((1, TQ), F32))

    def fill(t, _):
        o_ref[0, 0, t] = jnp.full((TQ, kb), NEG_INF, o_ref.dtype)
        return 0

    lax.fori_loop(nkb, nkbt, fill, 0)


def _dsa_mask(idx16, ik2, iwt, *, b, sq, sk, q0, sk_valid, topk, kb):
    nqt, nkbt = sq // TQ, sk // kb
    n_iq = N_IDX_HEADS * IDX_DIM
    return pl.pallas_call(
        functools.partial(_dsa_mask_kernel, q0=q0, kb=kb, nkbt=nkbt, sk_valid=sk_valid, topk=topk),
        grid=(b, nqt),
        in_specs=[pl.BlockSpec((TQ, n_iq), lambda bi, i: (bi * nqt + i, 0)),
                  pl.BlockSpec((sk, 2 * LANE), lambda bi, i: (bi, 0)),
                  pl.BlockSpec((N_IDX_HEADS, TQ), lambda bi, i: (0, bi * nqt + i))],
        out_specs=pl.BlockSpec((1, 1, nkbt, TQ, kb), lambda bi, i: (bi, i, 0, 0, 0)),
        out_shape=jax.ShapeDtypeStruct((b, nqt, nkbt, TQ, kb), BF16),
        scratch_shapes=[pltpu.VMEM((sk, TQ), jnp.int32)],
        compiler_params=_cparams(("parallel", "arbitrary")),
        name="dsa_mask",
    )(idx16, ik2, iwt)


def _gated(o, g):
    return o * (g * jax.nn.sigmoid(g))


def _head(g):
    return slice(g * HEAD_DIM, (g + 1) * HEAD_DIM)


def _dsa_attn_kernel(q_ref, k_ref, v_ref, m_ref, sl_ref, g_ref, o_ref, *, q0, kb, nkbt, gh):
    i = pl.program_id(2)
    tq = q_ref.shape[0]
    nkb = jnp.minimum(nkbt, (q0 + (i + 1) * tq + kb - 1) // kb)
    qpos = q0 + i * tq + lax.broadcasted_iota(jnp.int32, (tq, kb), 0)
    col = lax.broadcasted_iota(jnp.int32, (tq, kb), 1)

    def body(t, carry):
        off = pl.multiple_of(t * kb, kb)
        dist = jnp.abs(qpos - (off + col)).astype(F32)
        if tq > TQ:
            mask = jnp.concatenate([m_ref[0, u, t] for u in range(tq // TQ)], axis=0).astype(F32)
        else:
            mask = m_ref[0, 0, t, :tq, :].astype(F32)
        qk = [_nt_dot(q_ref[:, _head(g)], k_ref[pl.ds(off, kb), _head(g)]) for g in range(gh)]
        soft = []
        for g in range(gh):
            m, l, _ = carry[g]
            s = qk[g] - (sl_ref[g][:, :1] * LOG2E) * dist + mask
            m_new = jnp.maximum(m, s.max(axis=1, keepdims=True))
            m_use = jnp.where(m_new == NEG_INF, 0.0, m_new)
            alpha = jnp.exp2(m - m_use)
            p = jnp.exp2(s - m_use)
            soft.append((m_new, alpha * l + p.sum(axis=1, keepdims=True), alpha, p.astype(BF16)))
        pv = [jnp.dot(soft[g][3], v_ref[pl.ds(off, kb), _head(g)], preferred_element_type=F32)
              for g in range(gh)]
        return tuple((soft[g][0], soft[g][1], soft[g][2] * carry[g][2] + pv[g]) for g in range(gh))

    init = tuple((jnp.full((tq, 1), NEG_INF, F32), jnp.zeros((tq, 1), F32),
                  jnp.zeros((tq, HEAD_DIM), F32)) for _ in range(gh))
    res = lax.fori_loop(0, nkb, body, init)
    for g in range(gh):
        _, l, acc = res[g]
        o_ref[:, _head(g)] = _gated(acc / l, g_ref[:, _head(g)]).astype(o_ref.dtype)


def _stick_kernel(q_ref, k_ref, v_ref, g_ref, o_ref, *, q0, kb, nkbt, gh):
    i = pl.program_id(2)
    tq = q_ref.shape[0]
    nkb = jnp.minimum(nkbt, (q0 + (i + 1) * tq + kb - 1) // kb)
    from_here = (lax.broadcasted_iota(jnp.int32, (kb, kb), 0)
                 >= lax.broadcasted_iota(jnp.int32, (kb, kb), 1)).astype(BF16)
    qpos = q0 + i * tq + lax.broadcasted_iota(jnp.int32, (tq, kb), 0)
    col = lax.broadcasted_iota(jnp.int32, (tq, kb), 1)

    def block(off, carry, diagonal):
        causal = (off + col) < qpos
        qk = [_nt_dot(q_ref[:, _head(g)], k_ref[pl.ds(off, kb), _head(g)]) for g in range(gh)]
        drops = []
        for g in range(gh):
            z2 = qk[g]
            neg_abs = lax.bitcast_convert_type(lax.bitcast_convert_type(z2, jnp.int32) | INT_MIN, F32)
            drop = jnp.maximum(z2, 0.0) + jnp.log2(1.0 + jnp.exp2(neg_abs))
            if diagonal:
                drop = jnp.where(causal, drop, 0.0)
            drops.append(drop)
        incl = []
        for g in range(gh):
            hi = drops[g].astype(BF16)
            lo = (drops[g] - hi.astype(F32)).astype(BF16)
            incl.append(jnp.dot(hi, from_here, preferred_element_type=F32)
                        + jnp.dot(lo, from_here, preferred_element_type=F32))
        w = []
        for g in range(gh):
            wg = jnp.exp2(qk[g] - carry[g][0] - incl[g])
            if diagonal:
                wg = jnp.where(causal, wg, 0.0)
            w.append(wg.astype(BF16))
        pv = [jnp.dot(w[g], v_ref[pl.ds(off, kb), _head(g)], preferred_element_type=F32) for g in range(gh)]
        return tuple((carry[g][0] + incl[g][:, :1], carry[g][1] + pv[g]) for g in range(gh))

    init = tuple((jnp.zeros((tq, 1), F32), jnp.zeros((tq, HEAD_DIM), F32)) for _ in range(gh))
    first = block(pl.multiple_of((nkb - 1) * kb, kb), init, True)
    res = lax.fori_loop(1, nkb, lambda t, c: block(pl.multiple_of((nkb - 1 - t) * kb, kb), c, False), first)
    for g in range(gh):
        o_ref[:, _head(g)] = _gated(res[g][1], g_ref[:, _head(g)]).astype(o_ref.dtype)


def _band_tile(q_ref, k_ref, v_ref, tab_ref, g_ref, o_ref, *, first, rows, sk_valid, gh):
    tq = rows.stop - rows.start
    jabs = first // TQ
    back = BAND_BLOCKS - 1
    shift = back - jnp.minimum(jabs, back)
    s0 = pl.multiple_of(jnp.maximum(jabs - back, 0) * TQ, TQ)
    width = BAND_BLOCKS * TQ
    qpos = first + lax.broadcasted_iota(jnp.int32, (tq, width), 0)
    kpos = s0 + lax.broadcasted_iota(jnp.int32, (tq, width), 1)
    qc, kc = qpos >> CHUNK_SHIFT, kpos >> CHUNK_SHIFT
    allowed = (kc <= qc) & (kc >= qc - C_PAST_CHUNKS) & (kpos < sk_valid)
    qk = [_nt_dot(q_ref[rows, _head(g)], k_ref[pl.ds(s0, width), _head(g)]) for g in range(gh)]
    entry = [jnp.clip(r + shift - (back - 2), 0, 2) for r in range(BAND_BLOCKS)]
    soft = []
    for g in range(gh):
        bias = jnp.concatenate([tab_ref[g, entry[r], :tq, :] for r in range(BAND_BLOCKS)], axis=1)
        s = jnp.where(allowed, qk[g] + bias, NEG_INF)
        p = jnp.exp2(s - s.max(axis=1, keepdims=True))
        soft.append((p.sum(axis=1, keepdims=True), p.astype(BF16)))
    pv = [jnp.dot(soft[g][1], v_ref[pl.ds(s0, width), _head(g)], preferred_element_type=F32)
          for g in range(gh)]
    for g in range(gh):
        o_ref[rows, _head(g)] = _gated(pv[g] / soft[g][0], g_ref[rows, _head(g)]).astype(o_ref.dtype)


def _band_kernel(q_ref, k_ref, v_ref, tab_ref, g_ref, o_ref, *, q0, sk_valid, gh):
    i = pl.program_id(2)
    rows = q_ref.shape[0]
    tq = min(TQ, rows)
    for u in range(rows // tq):
        _band_tile(q_ref, k_ref, v_ref, tab_ref, g_ref, o_ref, first=q0 + i * rows + u * tq,
                   rows=slice(u * tq, (u + 1) * tq), sk_valid=sk_valid, gh=gh)


def _attention(kind, q, kv, gate, extra, *, b, sq, sk, nh, q0, sk_valid=None, kb=None, gh=8):
    tq = min(2 * TQ, sq)
    nqt = sq // tq
    assert nh % gh == 0 and sq % tq == 0 and q0 % TQ == 0 and tq % 16 == 0
    ng = nh // gh
    gw = gh * HEAD_DIM
    q_spec = pl.BlockSpec((tq, gw), lambda bi, h, i: (bi * nqt + i, h))
    k_spec = pl.BlockSpec((sk, gw), lambda bi, h, i: (bi, h))
    v_spec = pl.BlockSpec((sk, gw), lambda bi, h, i: (bi, ng + h))
    head_vec = pl.BlockSpec((gh, 1, LANE), lambda bi, h, i: (h, 0, 0))
    if kind == "dsa":
        mask, slopes = extra
        nkbt = sk // kb
        body = functools.partial(_dsa_attn_kernel, q0=q0, kb=kb, nkbt=nkbt, gh=gh)
        extra_specs = [pl.BlockSpec((1, max(tq // TQ, 1), nkbt, TQ, kb), lambda bi, h, i: (bi, i, 0, 0, 0)),
                       head_vec]
        extra_args = [mask, slopes]
    elif kind == "stick":
        assert kb % tq == 0 and q0 % tq == 0
        body = functools.partial(_stick_kernel, q0=q0, kb=kb, nkbt=sk // kb, gh=gh)
        extra_specs, extra_args = [], []
    else:
        body = functools.partial(_band_kernel, q0=q0, sk_valid=sk_valid, gh=gh)
        extra_specs = [pl.BlockSpec((gh, 3, TQ, TQ), lambda bi, h, i: (h, 0, 0, 0))]
        extra_args = [extra]
    return pl.pallas_call(
        body,
        grid=(b, ng, nqt),
        in_specs=[q_spec, k_spec, v_spec] + extra_specs + [q_spec],
        out_specs=q_spec,
        out_shape=jax.ShapeDtypeStruct((b * sq, nh * HEAD_DIM), BF16),
        compiler_params=_cparams(("parallel", "parallel", "arbitrary")),
        name=kind + "_attention",
    )(q, kv, kv, *extra_args, gate)


def _pad_q(a, b, nq, sq):
    if nq == sq:
        return a
    return jnp.pad(a.reshape(b, nq, -1), ((0, 0), (0, sq - nq), (0, 0))).reshape(b * sq, -1)


def _with_cache(cache, new, b, nq, sq, lead):
    width = new.shape[-1]
    parts = [jnp.zeros((b, lead, width), BF16)] if lead else []
    parts += [cache.reshape(b, cache.shape[1], width).astype(BF16), new.reshape(b, nq, width).astype(BF16),
              jnp.zeros((b, sq - nq, width), BF16)]
    return jnp.concatenate(parts, axis=1).reshape(-1, width)


def _kv_rows_kernel(c_ref, n_ref, o_ref, *, lead_blocks, cache_blocks, nq, nsub):
    r = pl.program_id(1)
    rows, width = o_ref.shape[1], o_ref.shape[2]

    @pl.when(r < lead_blocks)
    def _():
        o_ref[...] = jnp.zeros(o_ref.shape, o_ref.dtype)

    @pl.when((r >= lead_blocks) & (r < lead_blocks + cache_blocks))
    def _():
        by_sub = jnp.swapaxes(c_ref[...], 0, 1)
        for c in range(nsub):
            o_ref[0, :, c * LANE:(c + 1) * LANE] = by_sub[c].astype(o_ref.dtype)

    @pl.when(r == lead_blocks + cache_blocks)
    def _():
        o_ref[0, :nq, :] = n_ref[0]
        o_ref[0, nq:, :] = jnp.zeros((rows - nq, width), o_ref.dtype)


def _kv_rows(cache, new, *, b, nq, sq, lead):
    width = new.shape[-1]
    p = cache.shape[1]
    nsub = width // LANE
    assert sq == TQ and p % TQ == 0 and lead % TQ == 0 and nq % 16 == 0
    lead_blocks, cache_blocks = lead // TQ, p // TQ
    cache_block = lambda bi, r: (bi * cache_blocks + jnp.clip(r - lead_blocks, 0, cache_blocks - 1), 0, 0)
    return pl.pallas_call(
        functools.partial(_kv_rows_kernel, lead_blocks=lead_blocks, cache_blocks=cache_blocks, nq=nq, nsub=nsub),
        grid=(b, lead_blocks + cache_blocks + 1),
        in_specs=[pl.BlockSpec((TQ, nsub, LANE), cache_block),
                  pl.BlockSpec((1, nq, width), lambda bi, r: (bi, 0, 0))],
        out_specs=pl.BlockSpec((1, TQ, width), lambda bi, r: (bi, r, 0)),
        out_shape=jax.ShapeDtypeStruct((b, lead + p + sq, width), BF16),
        compiler_params=_cparams(("parallel", "arbitrary")),
        name="kv_rows",
    )(cache.reshape(b * p, nsub, LANE), new.reshape(b, nq, width)).reshape(-1, width)


def _even_layer(x, wts, cache, *, b, nq, q0):
    sq = -(-nq // TQ) * TQ
    na = wts["out_a"].shape[0] // HEAD_DIM
    nb = wts["out_b"].shape[0] // HEAD_DIM
    h = _rmsnorm(x, wts["norm"], BF16)
    qa = _mm([(h, wts["qa"])], [BF16], scale=Q_SCALE)
    kva32, kva16 = _mm([(h, wts["kva"])], [F32, BF16])
    ga = _mm([(h, wts["ga"])], [F32])
    idx32, idx16 = _mm([(h, wts["idx"])], [F32, BF16], tm=512, tn=wts["idx"].shape[1])
    qb = _mm([(h, wts["qb"])], [BF16], scale=Q_SCALE)
    kvb32, kvb16 = _mm([(h, wts["kvb"])], [F32, BF16])
    gb = _mm([(h, wts["gb"])], [F32])

    n_iq = N_IDX_HEADS * IDX_DIM
    ikn = _iknorm(idx32, wts["gk"], n_iq // LANE)
    iw = idx32[:, n_iq + IDX_DIM:n_iq + IDX_DIM + N_IDX_HEADS]
    iwt = _pad_q(iw, b, nq, sq).T

    if cache is None:
        sk = sk_valid = sq
        ka_all, kb_all = kva16, kvb16
        ik_all = ikn.reshape(b, sk, IDX_DIM).astype(BF16)
    else:
        a_kv, a_kidx, b_kv = cache
        sk, sk_valid = q0 + sq, q0 + nq
        ka_all = _kv_rows(a_kv, kva16, b=b, nq=nq, sq=sq, lead=0)
        kb_all = _kv_rows(b_kv, kvb16, b=b, nq=nq, sq=sq, lead=0)
        ik_all = _with_cache(a_kidx, ikn, b, nq, sq, 0).reshape(b, sk, IDX_DIM)
    zeros = jnp.zeros_like(ik_all)
    ik2 = jnp.concatenate([ik_all, zeros, zeros, ik_all], axis=-1).reshape(b * sk, 2 * LANE)

    pad = functools.partial(_pad_q, b=b, nq=nq, sq=sq)
    kb_d = _pick(sk, (512, 384, 256, 128))
    mask = _dsa_mask(pad(idx16), ik2, iwt, b=b, sq=sq, sk=sk, q0=q0, sk_valid=sk_valid,
                     topk=min(TOPK_MAX, sk_valid // 4), kb=kb_d)
    slopes = 2.0 ** (-8.0 * jnp.arange(1, na + 1, dtype=F32) / na)
    slopes = jnp.broadcast_to(slopes[:, None, None], (na, 1, LANE))
    oa = _attention("dsa", qa, ka_all, ga, (mask, slopes), b=b, sq=nq, sk=sk, nh=na, q0=q0, kb=kb_d, gh=4)
    ob = _attention("stick", qb, kb_all, gb, None, b=b, sq=nq, sk=sk, nh=nb, q0=q0,
                    kb=_pick(sk, (256, 384, 128)))
    x_new = _mm([(oa, wts["out_a"]), (ob, wts["out_b"])], [F32], res=x)
    return x_new, kva32, ikn, kvb32


def _odd_layer(x, wts, cache, *, b, nq, q0):
    sq = -(-nq // TQ) * TQ
    nc = wts["out"].shape[0] // HEAD_DIM
    h = _rmsnorm(x, wts["norm"], BF16)
    q = _mm([(h, wts["q"])], [BF16], scale=Q_SCALE)
    kv32, kv16 = _mm([(h, wts["kv"])], [F32, BF16])
    g = _mm([(h, wts["g"])], [F32])
    if cache is None:
        sk = sk_valid = sq
        kv_all = kv16
    else:
        sk, sk_valid = q0 + sq, q0 + nq
        kv_all = _kv_rows(cache, kv16, b=b, nq=nq, sq=sq, lead=q0 - cache.shape[1])
    o = _attention("band", q, kv_all, g, wts["tab"], b=b, sq=nq, sk=sk, nh=nc, q0=q0, sk_valid=sk_valid)
    x_new = _mm([(o, wts["out"])], [F32], res=x)
    return x_new, kv32


def _even_weights(norm, w_in, gk, w_out, d):
    wa = d // 2
    n_idx = N_IDX_HEADS * IDX_DIM + IDX_DIM + N_IDX_HEADS
    n_idx_pad = -(-n_idx // LANE) * LANE
    o_idx = 4 * wa
    o_b = o_idx + n_idx
    w16 = w_in.astype(BF16)
    w16_out = w_out.astype(BF16)
    w16_b = w16[:, o_b:]
    return {
        "norm": norm, "gk": gk,
        "qa": _cols(w16, 0, wa), "kva": _cols(w16, wa, 2 * wa), "ga": _cols(w16, 3 * wa, wa),
        "idx": jnp.pad(w16[:, o_idx:o_b], ((0, 0), (0, n_idx_pad - n_idx))),
        "qb": _cols(w16_b, 0, wa), "kvb": _cols(w16_b, wa, 2 * wa), "gb": _cols(w16_b, 3 * wa, wa),
        "out_a": w16_out[:wa], "out_b": w16_out[wa:],
    }


def _odd_weights(norm, w_in, rel_bias, w_out, d):
    nc = rel_bias.shape[0]
    w16 = w_in.astype(BF16)
    rb = rel_bias * LOG2E
    lo, hi = rb[:, :1], rb[:, 2 * REL_CLIP:]
    by_dist = jnp.concatenate([jnp.broadcast_to(hi, (nc, 2 * TQ - 1 - REL_CLIP)), rb[:, ::-1],
                               jnp.broadcast_to(lo, (nc, TQ - 1 - REL_CLIP))], axis=1)
    window = lambda base: jnp.stack([by_dist[:, 2 * TQ - 1 - base - t:3 * TQ - 1 - base - t]
                                     for t in range(TQ)], axis=1)
    far = jnp.broadcast_to(hi[:, :, None], (nc, TQ, TQ))
    return {
        "norm": norm,
        "q": _cols(w16, 0, d), "kv": _cols(w16, d, 2 * d), "g": _cols(w16, 3 * d, d),
        "tab": jnp.stack([far, window(TQ), window(0)], axis=1),
        "out": w_out.astype(BF16),
    }


def kernel(x_prompt, x_sample, cache_a_kv, cache_a_kidx, cache_b_kv, cache_c_kv, norm_e, w_in_e,
           idx_k_gain, w_out_e, norm_o, w_in_o, rel_bias_o, w_out_o, norm_f):
    bp, sp, d = x_prompt.shape
    bs, ts, _ = x_sample.shape
    p_len = cache_a_kv.shape[2]
    depth = norm_e.shape[0] + norm_o.shape[0]
    assert sp % TQ == 0 and p_len % TQ == 0 and ts <= TQ
    assert REL_CLIP <= CHUNK and TQ == 2 * CHUNK

    xp = x_prompt.reshape(bp * sp, d)
    xs = x_sample.reshape(bs * ts, d)
    rows_p = lambda a, *tail: a.reshape(bp, sp, *tail)
    rows_s = lambda a, *tail: a.reshape(bs, ts, *tail)
    outs = {k: [] for k in ("a_kv_p", "a_ix_p", "b_kv_p", "c_kv_p", "a_kv_s", "a_ix_s", "b_kv_s", "c_kv_s")}
    for layer in range(depth):
        j = layer // 2
        if layer % 2 == 0:
            wts = _even_weights(norm_e[j], w_in_e[j], idx_k_gain[j], w_out_e[j], d)
            na = wts["out_a"].shape[0] // HEAD_DIM
            nb = wts["out_b"].shape[0] // HEAD_DIM
            xp, akv, aix, bkv = _even_layer(xp, wts, None, b=bp, nq=sp, q0=0)
            xs, akv2, aix2, bkv2 = _even_layer(xs, wts, (cache_a_kv[j], cache_a_kidx[j], cache_b_kv[j]),
                                               b=bs, nq=ts, q0=p_len)
            outs["a_kv_p"].append(rows_p(akv, 2, na, HEAD_DIM))
            outs["a_ix_p"].append(rows_p(aix, IDX_DIM))
            outs["b_kv_p"].append(rows_p(bkv, 2, nb, HEAD_DIM))
            outs["a_kv_s"].append(rows_s(akv2, 2, na, HEAD_DIM))
            outs["a_ix_s"].append(rows_s(aix2, IDX_DIM))
            outs["b_kv_s"].append(rows_s(bkv2, 2, nb, HEAD_DIM))
        else:
            wts = _odd_weights(norm_o[j], w_in_o[j], rel_bias_o[j], w_out_o[j], d)
            nc = wts["out"].shape[0] // HEAD_DIM
            wc = min(C_PAST_CHUNKS * CHUNK, sp)
            xp, ckv = _odd_layer(xp, wts, None, b=bp, nq=sp, q0=0)
            xs, ckv2 = _odd_layer(xs, wts, cache_c_kv[j], b=bs, nq=ts, q0=p_len)
            outs["c_kv_p"].append(rows_p(ckv, 2, nc, HEAD_DIM)[:, sp - wc:])
            outs["c_kv_s"].append(rows_s(ckv2, 2, nc, HEAD_DIM))
    y_prompt = _rmsnorm(xp, norm_f, F32).reshape(bp, sp, d)
    y_sample = _rmsnorm(xs, norm_f, F32).reshape(bs, ts, d)
    return (y_prompt, y_sample,
            jnp.stack(outs["a_kv_p"]), jnp.stack(outs["a_ix_p"]), jnp.stack(outs["b_kv_p"]), jnp.stack(outs["c_kv_p"]),
            jnp.stack(outs["a_kv_s"]), jnp.stack(outs["a_ix_s"]), jnp.stack(outs["b_kv_s"]), jnp.stack(outs["c_kv_s"]))
```

```python
import functools

import jax
import jax.numpy as jnp
from jax import lax
from jax.experimental import pallas as pl
from jax.experimental.pallas import tpu as pltpu

F32 = jnp.float32
BF16 = jnp.bfloat16

HEAD_DIM = 128
CHUNK = 64
CHUNK_SHIFT = 6
N_IDX_HEADS = 16
IDX_DIM = 64
TOPK_MAX = 256
C_PAST_CHUNKS = 8
REL_CLIP = 64
EPS = 1e-6
ATT_SCALE = HEAD_DIM ** -0.5
LOG2E = 1.4426950408889634
Q_SCALE = ATT_SCALE * LOG2E
IDX_SCALE = (IDX_DIM ** -0.5) * (N_IDX_HEADS ** -0.5)

TQ = 128
LANE = 128
BAND_BLOCKS = (C_PAST_CHUNKS * CHUNK) // TQ + 1
INT_MIN = -2 ** 31
NEG_INF = float("-inf")
VMEM_LIMIT = 56 * 1024 * 1024


def _cparams(sem):
    return pltpu.CompilerParams(dimension_semantics=sem, vmem_limit_bytes=VMEM_LIMIT)


def _pick(n, candidates):
    for c in candidates:
        if n % c == 0:
            return c
    raise ValueError(f"no block size in {candidates} divides {n}")


def _rmsnorm_kernel(x_ref, g_ref, o_ref):
    x = x_ref[...]
    ms = jnp.mean(x * x, axis=-1, keepdims=True)
    o_ref[...] = (x * lax.rsqrt(ms + EPS) * g_ref[...]).astype(o_ref.dtype)


def _rmsnorm(x, g, out_dtype):
    m, d = x.shape
    tm = _pick(m, (512, 256, 128, 64, 32, 16, 8))
    return pl.pallas_call(
        _rmsnorm_kernel,
        grid=(m // tm,),
        in_specs=[pl.BlockSpec((tm, d), lambda i: (i, 0)),
                  pl.BlockSpec((1, d), lambda i: (0, 0))],
        out_specs=pl.BlockSpec((tm, d), lambda i: (i, 0)),
        out_shape=jax.ShapeDtypeStruct((m, d), out_dtype),
        compiler_params=_cparams(("parallel",)),
        name="rmsnorm",
    )(x, g.reshape(1, d))


def _iknorm_kernel(x_ref, g_ref, o_ref):
    x = x_ref[:, :IDX_DIM]
    ms = jnp.mean(x * x, axis=-1, keepdims=True)
    o_ref[...] = x * lax.rsqrt(ms + EPS) * g_ref[...]


def _iknorm(idx32, gk, col_block):
    m = idx32.shape[0]
    tm = _pick(m, (1024, 512, 256, 128, 64, 32, 16, 8))
    return pl.pallas_call(
        _iknorm_kernel,
        grid=(m // tm,),
        in_specs=[pl.BlockSpec((tm, LANE), lambda i: (i, col_block)),
                  pl.BlockSpec((1, IDX_DIM), lambda i: (0, 0))],
        out_specs=pl.BlockSpec((tm, IDX_DIM), lambda i: (i, 0)),
        out_shape=jax.ShapeDtypeStruct((m, IDX_DIM), F32),
        compiler_params=_cparams(("parallel",)),
        name="iknorm",
    )(idx32, gk.reshape(1, IDX_DIM))


def _mm_kernel(*refs, n_pairs, has_res, n_out, scale):
    acc = None
    for p in range(n_pairs):
        d = jnp.dot(refs[2 * p][...], refs[2 * p + 1][...], preferred_element_type=F32)
        acc = d if acc is None else acc + d
    pos = 2 * n_pairs
    if has_res:
        acc = acc + refs[pos][...]
        pos += 1
    if scale is not None:
        acc = acc * scale
    for o_ref in refs[pos:pos + n_out]:
        o_ref[...] = acc.astype(o_ref.dtype)


def _cols(w, lo, n):
    return (w, lo, n)


def _mm(pairs, out_dtypes, res=None, scale=None, tm=1024, tn=1024):
    pairs = [(a, w if isinstance(w, tuple) else (w, 0, w.shape[1])) for a, w in pairs]
    m = pairs[0][0].shape[0]
    n = pairs[0][1][2]
    tm = min(tm, m)
    tn = min(tn, n)
    assert m % tm == 0 and n % tn == 0, (m, n, tm, tn)
    in_specs, args = [], []
    for a, (w, lo, width) in pairs:
        k = a.shape[1]
        assert width == n and lo % tn == 0, (lo, width, tn)
        in_specs += [pl.BlockSpec((tm, k), lambda i, j: (i, 0)),
                     pl.BlockSpec((k, tn), lambda i, j, j0=lo // tn: (0, j0 + j))]
        args += [a, w]
    if res is not None:
        in_specs.append(pl.BlockSpec((tm, tn), lambda i, j: (i, j)))
        args.append(res)
    outs = pl.pallas_call(
        functools.partial(_mm_kernel, n_pairs=len(pairs), has_res=res is not None,
                          n_out=len(out_dtypes), scale=scale),
        grid=(m // tm, n // tn),
        in_specs=in_specs,
        out_specs=[pl.BlockSpec((tm, tn), lambda i, j: (i, j)) for _ in out_dtypes],
        out_shape=[jax.ShapeDtypeStruct((m, n), dt) for dt in out_dtypes],
        compiler_params=_cparams(("parallel", "arbitrary")),
        name="matmul",
    )(*args)
    return outs[0] if len(out_dtypes) == 1 else tuple(outs)


def _nt_dot(a, b):
    return lax.dot_general(a, b, (((1,), (1,)), ((), ())), preferred_element_type=F32)


def _dsa_mask_kernel(iq_ref, ik_ref, iwt_ref, o_ref, keys_ref, *, q0, kb, nkbt, sk_valid, topk):
    i = pl.program_id(1)
    nkb = jnp.minimum(nkbt, (q0 + (i + 1) * TQ + kb - 1) // kb)
    w = iwt_ref[...] * IDX_SCALE
    qchunk = (q0 + i * TQ + lax.broadcasted_iota(jnp.int32, (1, TQ), 1)) >> CHUNK_SHIFT

    def build(t, _):
        off = pl.multiple_of(t * kb, kb)
        acc = jnp.zeros((kb, TQ), F32)
        for h in range(N_IDX_HEADS):
            e, p = h % 2, h // 2
            ikh = ik_ref[pl.ds(off, kb), e * LANE:(e + 1) * LANE]
            s = _nt_dot(ikh, iq_ref[:, p * LANE:(p + 1) * LANE])
            acc = acc + jnp.maximum(s, 0.0) * w[h:h + 1, :]
        acc = acc + 0.0
        bits = lax.bitcast_convert_type(acc, jnp.int32)
        key = jnp.where(bits < 0, bits ^ 0x7FFFFFFF, bits)
        kpos = off + lax.broadcasted_iota(jnp.int32, (kb, TQ), 0)
        adm = ((kpos >> CHUNK_SHIFT) <= qchunk) & (kpos < sk_valid)
        keys_ref[pl.ds(off, kb), :] = jnp.where(adm, key, INT_MIN)
        return 0

    lax.fori_loop(0, nkb, build, 0)

    def count(pred):
        def body(t, acc8):
            off = pl.multiple_of(t * kb, kb)
            hit = pred(keys_ref[pl.ds(off, kb), :]).astype(jnp.int32)
            return acc8 + hit.reshape(kb // 8, 8, TQ).sum(axis=0)
        acc8 = lax.fori_loop(0, nkb, body, jnp.zeros((8, TQ), jnp.int32))
        return acc8.sum(axis=0, keepdims=True)

    thr = jnp.where(count(lambda kx: kx >= 0) >= topk, 0, INT_MIN).astype(jnp.int32)

    def bit_step(t, thr):
        cand = thr | (jnp.int32(1) << (30 - t))
        return jnp.where(count(lambda kx: kx >= cand) >= topk, cand, thr)

    thr = lax.fori_loop(0, 31, bit_step, thr)
    n_ge = count(lambda kx: kx >= thr)
    tie = jnp.max(jnp.where(thr == INT_MIN, 0, n_ge)) > topk

    @pl.when(jnp.logical_not(tie))
    def _():
        def emit(t, _):
            off = pl.multiple_of(t * kb, kb)
            kx = keys_ref[pl.ds(off, kb), :]
            sel = (kx >= thr) & (kx != INT_MIN)
            o_ref[0, 0, t] = jnp.where(sel, 0.0, NEG_INF).T.astype(o_ref.dtype)
            return 0

        lax.fori_loop(0, nkb, emit, 0)

    @pl.when(tie)
    def _():
        need = (topk - count(lambda kx: kx > thr)).astype(F32)
        tri = (lax.broadcasted_iota(jnp.int32, (kb, kb), 0)
               >= lax.broadcasted_iota(jnp.int32, (kb, kb), 1)).astype(BF16)

        def emit(t, seen):
            off = pl.multiple_of(t * kb, kb)
            kx = keys_ref[pl.ds(off, kb), :]
            eq = kx == thr
            rank = jnp.dot(tri, eq.astype(BF16), preferred_element_type=F32) + seen
            sel = ((kx > thr) | (eq & (rank <= need))) & (kx != INT_MIN)
            o_ref[0, 0, t] = jnp.where(sel, 0.0, NEG_INF).T.astype(o_ref.dtype)
            return rank[kb - 1:kb, :]

        lax.fori_loop(0, nkb, emit, jnp.zeros((1, TQ), F32))

    def fill(t, _):
        o_ref[0, 0, t] = jnp.full((TQ, kb), NEG_INF, o_ref.dtype)
        return 0

    lax.fori_loop(nkb, nkbt, fill, 0)


def _dsa_mask(idx16, ik2, iwt, *, b, sq, sk, q0, sk_valid, topk, kb):
    nqt, nkbt = sq // TQ, sk // kb
    n_iq = N_IDX_HEADS * IDX_DIM
    return pl.pallas_call(
        functools.partial(_dsa_mask_kernel, q0=q0, kb=kb, nkbt=nkbt, sk_valid=sk_valid, topk=topk),
        grid=(b, nqt),
        in_specs=[pl.BlockSpec((TQ, n_iq), lambda bi, i: (bi * nqt + i, 0)),
                  pl.BlockSpec((sk, 2 * LANE), lambda bi, i: (bi, 0)),
                  pl.BlockSpec((N_IDX_HEADS, TQ), lambda bi, i: (0, bi * nqt + i))],
        out_specs=pl.BlockSpec((1, 1, nkbt, TQ, kb), lambda bi, i: (bi, i, 0, 0, 0)),
        out_shape=jax.ShapeDtypeStruct((b, nqt, nkbt, TQ, kb), BF16),
        scratch_shapes=[pltpu.VMEM((sk, TQ), jnp.int32)],
        compiler_params=_cparams(("parallel", "arbitrary")),
        name="dsa_mask",
    )(idx16, ik2, iwt)


def _gated(o, g):
    return o * (g * jax.nn.sigmoid(g))


def _head(g):
    return slice(g * HEAD_DIM, (g + 1) * HEAD_DIM)


def _dsa_attn_kernel(q_ref, k_ref, v_ref, m_ref, sl_ref, g_ref, o_ref, *, q0, kb, nkbt, gh):
    i = pl.program_id(2)
    tq = q_ref.shape[0]
    nkb = jnp.minimum(nkbt, (q0 + (i + 1) * tq + kb - 1) // kb)
    qpos = q0 + i * tq + lax.broadcasted_iota(jnp.int32, (tq, kb), 0)
    col = lax.broadcasted_iota(jnp.int32, (tq, kb), 1)

    def body(t, carry):
        off = pl.multiple_of(t * kb, kb)
        dist = jnp.abs(qpos - (off + col)).astype(F32)
        if tq > TQ:
            mask = jnp.concatenate([m_ref[0, u, t] for u in range(tq // TQ)], axis=0).astype(F32)
        else:
            mask = m_ref[0, 0, t, :tq, :].astype(F32)
        qk = [_nt_dot(q_ref[:, _head(g)], k_ref[pl.ds(off, kb), _head(g)]) for g in range(gh)]
        soft = []
        for g in range(gh):
            m, l, _ = carry[g]
            s = qk[g] - (sl_ref[g][:, :1] * LOG2E) * dist + mask
            m_new = jnp.maximum(m, s.max(axis=1, keepdims=True))
            m_use = jnp.where(m_new == NEG_INF, 0.0, m_new)
            alpha = jnp.exp2(m - m_use)
            p = jnp.exp2(s - m_use)
            soft.append((m_new, alpha * l + p.sum(axis=1, keepdims=True), alpha, p.astype(BF16)))
        pv = [jnp.dot(soft[g][3], v_ref[pl.ds(off, kb), _head(g)], preferred_element_type=F32)
              for g in range(gh)]
        return tuple((soft[g][0], soft[g][1], soft[g][2] * carry[g][2] + pv[g]) for g in range(gh))

    init = tuple((jnp.full((tq, 1), NEG_INF, F32), jnp.zeros((tq, 1), F32),
                  jnp.zeros((tq, HEAD_DIM), F32)) for _ in range(gh))
    res = lax.fori_loop(0, nkb, body, init)
    for g in range(gh):
        _, l, acc = res[g]
        o_ref[:, _head(g)] = _gated(acc / l, g_ref[:, _head(g)]).astype(o_ref.dtype)


def _stick_kernel(q_ref, k_ref, v_ref, g_ref, o_ref, *, q0, kb, nkbt, gh):
    i = pl.program_id(2)
    tq = q_ref.shape[0]
    nkb = jnp.minimum(nkbt, (q0 + (i + 1) * tq + kb - 1) // kb)
    from_here = (lax.broadcasted_iota(jnp.int32, (kb, kb), 0)
                 >= lax.broadcasted_iota(jnp.int32, (kb, kb), 1)).astype(BF16)
    qpos = q0 + i * tq + lax.broadcasted_iota(jnp.int32, (tq, kb), 0)
    col = lax.broadcasted_iota(jnp.int32, (tq, kb), 1)

    def block(off, carry, diagonal):
        causal = (off + col) < qpos
        qk = [_nt_dot(q_ref[:, _head(g)], k_ref[pl.ds(off, kb), _head(g)]) for g in range(gh)]
        drops = []
        for g in range(gh):
            z2 = qk[g]
            neg_abs = lax.bitcast_convert_type(lax.bitcast_convert_type(z2, jnp.int32) | INT_MIN, F32)
            drop = jnp.maximum(z2, 0.0) + jnp.log2(1.0 + jnp.exp2(neg_abs))
            if diagonal:
                drop = jnp.where(causal, drop, 0.0)
            drops.append(drop)
        incl = []
        for g in range(gh):
            hi = drops[g].astype(BF16)
            lo = (drops[g] - hi.astype(F32)).astype(BF16)
            incl.append(jnp.dot(hi, from_here, preferred_element_type=F32)
                        + jnp.dot(lo, from_here, preferred_element_type=F32))
        w = []
        for g in range(gh):
            wg = jnp.exp2(qk[g] - carry[g][0] - incl[g])
            if diagonal:
                wg = jnp.where(causal, wg, 0.0)
            w.append(wg.astype(BF16))
        pv = [jnp.dot(w[g], v_ref[pl.ds(off, kb), _head(g)], preferred_element_type=F32) for g in range(gh)]
        return tuple((carry[g][0] + incl[g][:, :1], carry[g][1] + pv[g]) for g in range(gh))

    init = tuple((jnp.zeros((tq, 1), F32), jnp.zeros((tq, HEAD_DIM), F32)) for _ in range(gh))
    first = block(pl.multiple_of((nkb - 1) * kb, kb), init, True)
    res = lax.fori_loop(1, nkb, lambda t, c: block(pl.multiple_of((nkb - 1 - t) * kb, kb), c, False), first)
    for g in range(gh):
        o_ref[:, _head(g)] = _gated(res[g][1], g_ref[:, _head(g)]).astype(o_ref.dtype)


def _band_tile(q_ref, k_ref, v_ref, tab_ref, g_ref, o_ref, *, first, rows, sk_valid, gh):
    tq = rows.stop - rows.start
    jabs = first // TQ
    back = BAND_BLOCKS - 1
    shift = back - jnp.minimum(jabs, back)
    s0 = pl.multiple_of(jnp.maximum(jabs - back, 0) * TQ, TQ)
    width = BAND_BLOCKS * TQ
    qpos = first + lax.broadcasted_iota(jnp.int32, (tq, width), 0)
    kpos = s0 + lax.broadcasted_iota(jnp.int32, (tq, width), 1)
    qc, kc = qpos >> CHUNK_SHIFT, kpos >> CHUNK_SHIFT
    allowed = (kc <= qc) & (kc >= qc - C_PAST_CHUNKS) & (kpos < sk_valid)
    qk = [_nt_dot(q_ref[rows, _head(g)], k_ref[pl.ds(s0, width), _head(g)]) for g in range(gh)]
    entry = [jnp.clip(r + shift - (back - 2), 0, 2) for r in range(BAND_BLOCKS)]
    soft = []
    for g in range(gh):
        bias = jnp.concatenate([tab_ref[g, entry[r], :tq, :] for r in range(BAND_BLOCKS)], axis=1)
        s = jnp.where(allowed, qk[g] + bias, NEG_INF)
        p = jnp.exp2(s - s.max(axis=1, keepdims=True))
        soft.append((p.sum(axis=1, keepdims=True), p.astype(BF16)))
    pv = [jnp.dot(soft[g][1], v_ref[pl.ds(s0, width), _head(g)], preferred_element_type=F32)
          for g in range(gh)]
    for g in range(gh):
        o_ref[rows, _head(g)] = _gated(pv[g] / soft[g][0], g_ref[rows, _head(g)]).astype(o_ref.dtype)


def _band_kernel(q_ref, k_ref, v_ref, tab_ref, g_ref, o_ref, *, q0, sk_valid, gh):
    i = pl.program_id(2)
    rows = q_ref.shape[0]
    tq = min(TQ, rows)
    for u in range(rows // tq):
        _band_tile(q_ref, k_ref, v_ref, tab_ref, g_ref, o_ref, first=q0 + i * rows + u * tq,
                   rows=slice(u * tq, (u + 1) * tq), sk_valid=sk_valid, gh=gh)


def _attention(kind, q, kv, gate, extra, *, b, sq, sk, nh, q0, sk_valid=None, kb=None, gh=8):
    tq = min((4 if kind == "band" else 2) * TQ, sq)
    nqt = sq // tq
    assert nh % gh == 0 and sq % tq == 0 and q0 % TQ == 0 and tq % 16 == 0
    ng = nh // gh
    gw = gh * HEAD_DIM
    q_spec = pl.BlockSpec((tq, gw), lambda bi, h, i: (bi * nqt + i, h))
    k_spec = pl.BlockSpec((sk, gw), lambda bi, h, i: (bi, h))
    v_spec = pl.BlockSpec((sk, gw), lambda bi, h, i: (bi, ng + h))
    head_vec = pl.BlockSpec((gh, 1, LANE), lambda bi, h, i: (h, 0, 0))
    if kind == "dsa":
        mask, slopes = extra
        nkbt = sk // kb
        body = functools.partial(_dsa_attn_kernel, q0=q0, kb=kb, nkbt=nkbt, gh=gh)
        extra_specs = [pl.BlockSpec((1, max(tq // TQ, 1), nkbt, TQ, kb), lambda bi, h, i: (bi, i, 0, 0, 0)),
                       head_vec]
        extra_args = [mask, slopes]
    elif kind == "stick":
        assert kb % tq == 0 and q0 % tq == 0
        body = functools.partial(_stick_kernel, q0=q0, kb=kb, nkbt=sk // kb, gh=gh)
        extra_specs, extra_args = [], []
    else:
        body = functools.partial(_band_kernel, q0=q0, sk_valid=sk_valid, gh=gh)
        extra_specs = [pl.BlockSpec((gh, 3, TQ, TQ), lambda bi, h, i: (h, 0, 0, 0))]
        extra_args = [extra]
    return pl.pallas_call(
        body,
        grid=(b, ng, nqt),
        in_specs=[q_spec, k_spec, v_spec] + extra_specs + [q_spec],
        out_specs=q_spec,
        out_shape=jax.ShapeDtypeStruct((b * sq, nh * HEAD_DIM), BF16),
        compiler_params=_cparams(("parallel", "parallel", "arbitrary")),
        name=kind + "_attention",
    )(q, kv, kv, *extra_args, gate)


def _pad_q(a, b, nq, sq):
    if nq == sq:
        return a
    return jnp.pad(a.reshape(b, nq, -1), ((0, 0), (0, sq - nq), (0, 0))).reshape(b * sq, -1)


def _with_cache(cache, new, b, nq, sq, lead):
    width = new.shape[-1]
    parts = [jnp.zeros((b, lead, width), BF16)] if lead else []
    parts += [cache.reshape(b, cache.shape[1], width).astype(BF16), new.reshape(b, nq, width).astype(BF16),
              jnp.zeros((b, sq - nq, width), BF16)]
    return jnp.concatenate(parts, axis=1).reshape(-1, width)


def _kv_rows_kernel(c_ref, n_ref, o_ref, *, lead_blocks, cache_blocks, nq, nsub):
    r = pl.program_id(1)
    rows, width = o_ref.shape[1], o_ref.shape[2]

    @pl.when(r < lead_blocks)
    def _():
        o_ref[...] = jnp.zeros(o_ref.shape, o_ref.dtype)

    @pl.when((r >= lead_blocks) & (r < lead_blocks + cache_blocks))
    def _():
        by_sub = jnp.swapaxes(c_ref[...], 0, 1)
        for c in range(nsub):
            o_ref[0, :, c * LANE:(c + 1) * LANE] = by_sub[c].astype(o_ref.dtype)

    @pl.when(r == lead_blocks + cache_blocks)
    def _():
        o_ref[0, :nq, :] = n_ref[0]
        o_ref[0, nq:, :] = jnp.zeros((rows - nq, width), o_ref.dtype)


def _kv_rows(cache, new, *, b, nq, sq, lead):
    width = new.shape[-1]
    p = cache.shape[1]
    nsub = width // LANE
    assert sq == TQ and p % TQ == 0 and lead % TQ == 0 and nq % 16 == 0
    lead_blocks, cache_blocks = lead // TQ, p // TQ
    cache_block = lambda bi, r: (bi * cache_blocks + jnp.clip(r - lead_blocks, 0, cache_blocks - 1), 0, 0)
    return pl.pallas_call(
        functools.partial(_kv_rows_kernel, lead_blocks=lead_blocks, cache_blocks=cache_blocks, nq=nq, nsub=nsub),
        grid=(b, lead_blocks + cache_blocks + 1),
        in_specs=[pl.BlockSpec((TQ, nsub, LANE), cache_block),
                  pl.BlockSpec((1, nq, width), lambda bi, r: (bi, 0, 0))],
        out_specs=pl.BlockSpec((1, TQ, width), lambda bi, r: (bi, r, 0)),
        out_shape=jax.ShapeDtypeStruct((b, lead + p + sq, width), BF16),
        compiler_params=_cparams(("parallel", "arbitrary")),
        name="kv_rows",
    )(cache.reshape(b * p, nsub, LANE), new.reshape(b, nq, width)).reshape(-1, width)


def _even_layer(x, wts, cache, *, b, nq, q0):
    sq = -(-nq // TQ) * TQ
    na = wts["out_a"].shape[0] // HEAD_DIM
    nb = wts["out_b"].shape[0] // HEAD_DIM
    h = _rmsnorm(x, wts["norm"], BF16)
    qa = _mm([(h, wts["qa"])], [BF16], scale=Q_SCALE)
    kva32, kva16 = _mm([(h, wts["kva"])], [F32, BF16])
    ga = _mm([(h, wts["ga"])], [F32])
    idx32, idx16 = _mm([(h, wts["idx"])], [F32, BF16], tm=512, tn=wts["idx"].shape[1])
    qb = _mm([(h, wts["qb"])], [BF16], scale=Q_SCALE)
    kvb32, kvb16 = _mm([(h, wts["kvb"])], [F32, BF16])
    gb = _mm([(h, wts["gb"])], [F32])

    n_iq = N_IDX_HEADS * IDX_DIM
    ikn = _iknorm(idx32, wts["gk"], n_iq // LANE)
    iw = idx32[:, n_iq + IDX_DIM:n_iq + IDX_DIM + N_IDX_HEADS]
    iwt = _pad_q(iw, b, nq, sq).T

    if cache is None:
        sk = sk_valid = sq
        ka_all, kb_all = kva16, kvb16
        ik_all = ikn.reshape(b, sk, IDX_DIM).astype(BF16)
    else:
        a_kv, a_kidx, b_kv = cache
        sk, sk_valid = q0 + sq, q0 + nq
        ka_all = _kv_rows(a_kv, kva16, b=b, nq=nq, sq=sq, lead=0)
        kb_all = _kv_rows(b_kv, kvb16, b=b, nq=nq, sq=sq, lead=0)
        ik_all = _with_cache(a_kidx, ikn, b, nq, sq, 0).reshape(b, sk, IDX_DIM)
    zeros = jnp.zeros_like(ik_all)
    ik2 = jnp.concatenate([ik_all, zeros, zeros, ik_all], axis=-1).reshape(b * sk, 2 * LANE)

    pad = functools.partial(_pad_q, b=b, nq=nq, sq=sq)
    kb_d = _pick(sk, (512, 384, 256, 128))
    mask = _dsa_mask(pad(idx16), ik2, iwt, b=b, sq=sq, sk=sk, q0=q0, sk_valid=sk_valid,
                     topk=min(TOPK_MAX, sk_valid // 4), kb=kb_d)
    slopes = 2.0 ** (-8.0 * jnp.arange(1, na + 1, dtype=F32) / na)
    slopes = jnp.broadcast_to(slopes[:, None, None], (na, 1, LANE))
    oa = _attention("dsa", qa, ka_all, ga, (mask, slopes), b=b, sq=nq, sk=sk, nh=na, q0=q0, kb=kb_d, gh=4)
    ob = _attention("stick", qb, kb_all, gb, None, b=b, sq=nq, sk=sk, nh=nb, q0=q0,
                    kb=_pick(sk, (256, 384, 128)))
    x_new = _mm([(oa, wts["out_a"]), (ob, wts["out_b"])], [F32], res=x)
    return x_new, kva32, ikn, kvb32


def _odd_layer(x, wts, cache, *, b, nq, q0):
    sq = -(-nq // TQ) * TQ
    nc = wts["out"].shape[0] // HEAD_DIM
    h = _rmsnorm(x, wts["norm"], BF16)
    q = _mm([(h, wts["q"])], [BF16], scale=Q_SCALE)
    kv32, kv16 = _mm([(h, wts["kv"])], [F32, BF16])
    g = _mm([(h, wts["g"])], [F32])
    if cache is None:
        sk = sk_valid = sq
        kv_all = kv16
    else:
        sk, sk_valid = q0 + sq, q0 + nq
        kv_all = _kv_rows(cache, kv16, b=b, nq=nq, sq=sq, lead=q0 - cache.shape[1])
    o = _attention("band", q, kv_all, g, wts["tab"], b=b, sq=nq, sk=sk, nh=nc, q0=q0, sk_valid=sk_valid)
    x_new = _mm([(o, wts["out"])], [F32], res=x)
    return x_new, kv32


def _even_weights(norm, w_in, gk, w_out, d):
    wa = d // 2
    n_idx = N_IDX_HEADS * IDX_DIM + IDX_DIM + N_IDX_HEADS
    n_idx_pad = -(-n_idx // LANE) * LANE
    o_idx = 4 * wa
    o_b = o_idx + n_idx
    w16 = w_in.astype(BF16)
    w16_out = w_out.astype(BF16)
    w16_b = w16[:, o_b:]
    return {
        "norm": norm, "gk": gk,
        "qa": _cols(w16, 0, wa), "kva": _cols(w16, wa, 2 * wa), "ga": _cols(w16, 3 * wa, wa),
        "idx": jnp.pad(w16[:, o_idx:o_b], ((0, 0), (0, n_idx_pad - n_idx))),
        "qb": _cols(w16_b, 0, wa), "kvb": _cols(w16_b, wa, 2 * wa), "gb": _cols(w16_b, 3 * wa, wa),
        "out_a": w16_out[:wa], "out_b": w16_out[wa:],
    }


def _odd_weights(norm, w_in, rel_bias, w_out, d):
    nc = rel_bias.shape[0]
    w16 = w_in.astype(BF16)
    rb = rel_bias * LOG2E
    lo, hi = rb[:, :1], rb[:, 2 * REL_CLIP:]
    by_dist = jnp.concatenate([jnp.broadcast_to(hi, (nc, 2 * TQ - 1 - REL_CLIP)), rb[:, ::-1],
                               jnp.broadcast_to(lo, (nc, TQ - 1 - REL_CLIP))], axis=1)
    window = lambda base: jnp.stack([by_dist[:, 2 * TQ - 1 - base - t:3 * TQ - 1 - base - t]
                                     for t in range(TQ)], axis=1)
    far = jnp.broadcast_to(hi[:, :, None], (nc, TQ, TQ))
    return {
        "norm": norm,
        "q": _cols(w16, 0, d), "kv": _cols(w16, d, 2 * d), "g": _cols(w16, 3 * d, d),
        "tab": jnp.stack([far, window(TQ), window(0)], axis=1),
        "out": w_out.astype(BF16),
    }


def kernel(x_prompt, x_sample, cache_a_kv, cache_a_kidx, cache_b_kv, cache_c_kv, norm_e, w_in_e,
           idx_k_gain, w_out_e, norm_o, w_in_o, rel_bias_o, w_out_o, norm_f):
    bp, sp, d = x_prompt.shape
    bs, ts, _ = x_sample.shape
    p_len = cache_a_kv.shape[2]
    depth = norm_e.shape[0] + norm_o.shape[0]
    assert sp % TQ == 0 and p_len % TQ == 0 and ts <= TQ
    assert REL_CLIP <= CHUNK and TQ == 2 * CHUNK

    xp = x_prompt.reshape(bp * sp, d)
    xs = x_sample.reshape(bs * ts, d)
    rows_p = lambda a, *tail: a.reshape(bp, sp, *tail)
    rows_s = lambda a, *tail: a.reshape(bs, ts, *tail)
    outs = {k: [] for k in ("a_kv_p", "a_ix_p", "b_kv_p", "c_kv_p", "a_kv_s", "a_ix_s", "b_kv_s", "c_kv_s")}
    for layer in range(depth):
        j = layer // 2
        if layer % 2 == 0:
            wts = _even_weights(norm_e[j], w_in_e[j], idx_k_gain[j], w_out_e[j], d)
            na = wts["out_a"].shape[0] // HEAD_DIM
            nb = wts["out_b"].shape[0] // HEAD_DIM
            xp, akv, aix, bkv = _even_layer(xp, wts, None, b=bp, nq=sp, q0=0)
            xs, akv2, aix2, bkv2 = _even_layer(xs, wts, (cache_a_kv[j], cache_a_kidx[j], cache_b_kv[j]),
                                               b=bs, nq=ts, q0=p_len)
            outs["a_kv_p"].append(rows_p(akv, 2, na, HEAD_DIM))
            outs["a_ix_p"].append(rows_p(aix, IDX_DIM))
            outs["b_kv_p"].append(rows_p(bkv, 2, nb, HEAD_DIM))
            outs["a_kv_s"].append(rows_s(akv2, 2, na, HEAD_DIM))
            outs["a_ix_s"].append(rows_s(aix2, IDX_DIM))
            outs["b_kv_s"].append(rows_s(bkv2, 2, nb, HEAD_DIM))
        else:
            wts = _odd_weights(norm_o[j], w_in_o[j], rel_bias_o[j], w_out_o[j], d)
            nc = wts["out"].shape[0] // HEAD_DIM
            wc = min(C_PAST_CHUNKS * CHUNK, sp)
            xp, ckv = _odd_layer(xp, wts, None, b=bp, nq=sp, q0=0)
            xs, ckv2 = _odd_layer(xs, wts, cache_c_kv[j], b=bs, nq=ts, q0=p_len)
            outs["c_kv_p"].append(rows_p(ckv, 2, nc, HEAD_DIM)[:, sp - wc:])
            outs["c_kv_s"].append(rows_s(ckv2, 2, nc, HEAD_DIM))
    y_prompt = _rmsnorm(xp, norm_f, F32).reshape(bp, sp, d)
    y_sample = _rmsnorm(xs, norm_f, F32).reshape(bs, ts, d)
    return (y_prompt, y_sample,
            jnp.stack(outs["a_kv_p"]), jnp.stack(outs["a_ix_p"]), jnp.stack(outs["b_kv_p"]), jnp.stack(outs["c_kv_p"]),
            jnp.stack(outs["a_kv_s"]), jnp.stack(outs["a_ix_s"]), jnp.stack(outs["b_kv_s"]), jnp.stack(outs["c_kv_s"]))
```
